```python
import math
import jax, jax.numpy as jnp
from jax import lax
import numpy as np

D_MODEL = 1024
BATCH = 8
SEQ = 4096
DEPTH = 2

N_EVEN = (DEPTH + 1) // 2
N_ODD = DEPTH // 2

A_QK_DIM = 64
A_V_DIM = 2 * A_QK_DIM
A_WIDTH = D_MODEL // 2
A_HEADS = A_WIDTH // A_V_DIM
ROPE_DIM = A_QK_DIM // 4
ROPE_THETA = 500000.0
Q_BLOCK = 128

B_WIDTH = D_MODEL // 2
B_HEADS = 8
B_HEAD_DIM = B_WIDTH // B_HEADS
CONV_WIDTH = 4
LRU_C = 8.0

IN_WIDTH = 3 * A_WIDTH + 2 * B_WIDTH

POOL_WINDOWS = (2, 4, 8, 16)
POOL_GROUPS = len(POOL_WINDOWS)
POOL_GROUP_DIM = D_MODEL // POOL_GROUPS

FFN_DIM = 2816
N_EXPERTS = 8
TOP_K = 2
EXPERT_DIM = 3584

NORM_EPS = 1e-6
SUBLN_EPS = 1e-5

kernel_name = "hybrid_diffattn_rglru_pool_moe"


def rms_norm(x, g, eps=NORM_EPS):
    xf = x.astype(jnp.float32)
    y = xf * lax.rsqrt(jnp.mean(xf * xf, axis=-1, keepdims=True) + eps)
    return (y * g.astype(jnp.float32)).astype(x.dtype)


def rope_tables(seq):
    inv = ROPE_THETA ** (-jnp.arange(0, ROPE_DIM, 2, dtype=jnp.float32) / ROPE_DIM)
    ang = jnp.arange(seq, dtype=jnp.float32)[:, None] * inv[None, :]
    return jnp.cos(ang), jnp.sin(ang)


def apply_partial_rope(x, cos, sin):
    half = ROPE_DIM // 2
    x1 = x[..., :half].astype(jnp.float32)
    x2 = x[..., half:ROPE_DIM].astype(jnp.float32)
    c = cos[None, :, None, None, :]
    s = sin[None, :, None, None, :]
    rot = jnp.concatenate([x1 * c - x2 * s, x2 * c + x1 * s], axis=-1).astype(x.dtype)
    return jnp.concatenate([rot, x[..., ROPE_DIM:]], axis=-1)


def diff_attention(q, k, v, lam, lam_init, subln_g):
    bsz, seq = q.shape[0], q.shape[1]
    nb = seq // Q_BLOCK
    q = q * (A_QK_DIM ** -0.5)
    qb = q.reshape(bsz, nb, Q_BLOCK, A_HEADS, 2, A_QK_DIM).transpose(1, 0, 2, 3, 4, 5)
    key_pos = jnp.arange(seq)

    def block(args):
        q_blk, bi = args
        s = jnp.einsum("bqhmd,bkhmd->bhmqk", q_blk, k).astype(jnp.float32)
        q_pos = bi * Q_BLOCK + jnp.arange(Q_BLOCK)
        mask = key_pos[None, :] <= q_pos[:, None]
        p = jax.nn.softmax(jnp.where(mask, s, -jnp.inf), axis=-1)
        w = p[:, :, 0] - lam * p[:, :, 1]
        return jnp.einsum("bhqk,bkhd->bqhd", w.astype(v.dtype), v)

    o = lax.map(block, (qb, jnp.arange(nb)))
    o = o.transpose(1, 0, 2, 3, 4).reshape(bsz, seq, A_HEADS, A_V_DIM)
    o = rms_norm(o, subln_g, SUBLN_EPS) * (1.0 - lam_init)
    return o.reshape(bsz, seq, A_WIDTH)


def causal_depthwise_conv(x, w, b):
    y = lax.conv_general_dilated(
        x, w[:, None, :], window_strides=(1,), padding=[(CONV_WIDTH - 1, 0)],
        dimension_numbers=("NWC", "WIO", "NWC"), feature_group_count=x.shape[-1])
    return y + b


def rg_lru(x, wa, ba, wx, bx, lam):
    bsz, seq, ch = x.shape
    xh = x.reshape(bsz, seq, B_HEADS, B_HEAD_DIM)
    r = jax.nn.sigmoid((jnp.einsum("bshi,hij->bshj", xh, wa) + ba).astype(jnp.float32)).reshape(bsz, seq, ch)
    i = jax.nn.sigmoid((jnp.einsum("bshi,hij->bshj", xh, wx) + bx).astype(jnp.float32)).reshape(bsz, seq, ch)
    log_a = -LRU_C * r * jax.nn.softplus(-lam.astype(jnp.float32))
    a = jnp.exp(log_a)
    u = jnp.sqrt(-jnp.expm1(2.0 * log_a)) * (i * x.astype(jnp.float32))

    def combine(left, right):
        a1, b1 = left
        a2, b2 = right
        return a1 * a2, a2 * b1 + b2

    _, h = lax.associative_scan(combine, (a, u), axis=1)
    return h.astype(x.dtype)


def even_mixer(h, w_in, lq1, lk1, lq2, lk2, subln_g, conv_w, conv_b,
               wa, ba, wx, bx, lam_p, w_out, lam_init, cos, sin):
    bsz, seq, _ = h.shape
    proj = h @ w_in
    q, k, v, xb, gb = jnp.split(
        proj, [A_WIDTH, 2 * A_WIDTH, 3 * A_WIDTH, 3 * A_WIDTH + B_WIDTH], axis=-1)
    q = apply_partial_rope(q.reshape(bsz, seq, A_HEADS, 2, A_QK_DIM), cos, sin)
    k = apply_partial_rope(k.reshape(bsz, seq, A_HEADS, 2, A_QK_DIM), cos, sin)
    v = v.reshape(bsz, seq, A_HEADS, A_V_DIM)
    lam = (jnp.exp(jnp.sum(lq1.astype(jnp.float32) * lk1.astype(jnp.float32)))
           - jnp.exp(jnp.sum(lq2.astype(jnp.float32) * lk2.astype(jnp.float32)))
           + lam_init)
    attn = diff_attention(q, k, v, lam, lam_init, subln_g)
    rec = rg_lru(causal_depthwise_conv(xb, conv_w, conv_b), wa, ba, wx, bx, lam_p)
    rec = rec * jax.nn.gelu(gb)
    return jnp.concatenate([attn, rec], axis=-1) @ w_out


def pool_mixer(h, pool_w, pool_scale):
    bsz, seq, _ = h.shape
    hg = h.reshape(bsz, seq, POOL_GROUPS, POOL_GROUP_DIM)
    pos = jnp.arange(seq)
    outs = []
    for g, w in enumerate(POOL_WINDOWS):
        xg = hg[:, :, g].astype(jnp.float32)
        cs = jnp.cumsum(xg, axis=1)
        lower = jnp.pad(cs, ((0, 0), (w, 0), (0, 0)))[:, :seq]
        count = jnp.minimum(pos + 1, w).astype(jnp.float32)[None, :, None]
        outs.append((cs - lower) / count - xg)
    d = jnp.stack(outs, axis=2).astype(h.dtype)
    y = jnp.einsum("bsgc,gcd->bsgd", d, pool_w).reshape(bsz, seq, D_MODEL)
    return y * pool_scale


def swiglu(h, wg, wu, wd):
    return (jax.nn.silu(h @ wg) * (h @ wu)) @ wd


def moe_swiglu(h, router_w, wg, wu, wd):
    bsz, seq, dm = h.shape
    t = h.reshape(-1, dm)
    logits = (t @ router_w).astype(jnp.float32)
    top_val, top_idx = lax.top_k(logits, TOP_K)
    gates = jax.nn.softmax(top_val, axis=-1)
    combine = jnp.sum(jax.nn.one_hot(top_idx, N_EXPERTS, dtype=jnp.float32) * gates[..., None], axis=1)
    out = jnp.zeros_like(t)
    for e in range(N_EXPERTS):
        he = jax.nn.silu(t @ wg[e]) * (t @ wu[e])
        out = out + (combine[:, e:e + 1].astype(t.dtype) * he) @ wd[e]
    return out.reshape(bsz, seq, dm)


def setup_inputs(seed: int = 0) -> dict:
    key = jax.random.key(seed)
    ks = iter(jax.random.split(key, 40))
    f32 = jnp.float32

    def nrm(shape, scale):
        return jax.random.normal(next(ks), shape, f32) * scale

    def gain(shape):
        return 1.0 + 0.02 * jax.random.normal(next(ks), shape, f32)

    x = jax.random.normal(next(ks), (BATCH, SEQ, D_MODEL), f32)
    u = jax.random.uniform(next(ks), (N_EVEN, B_WIDTH), f32, 0.9, 0.999)
    s = u ** (1.0 / LRU_C)
    rg_lam = jnp.log(s / (1.0 - s))
    return {
        "x": x,
        "ln_mix_even": gain((N_EVEN, D_MODEL)),
        "w_in_even": nrm((N_EVEN, D_MODEL, IN_WIDTH), D_MODEL ** -0.5),
        "lam_q1": nrm((N_EVEN, A_QK_DIM), 0.1),
        "lam_k1": nrm((N_EVEN, A_QK_DIM), 0.1),
        "lam_q2": nrm((N_EVEN, A_QK_DIM), 0.1),
        "lam_k2": nrm((N_EVEN, A_QK_DIM), 0.1),
        "subln_g": gain((N_EVEN, A_V_DIM)),
        "conv_w": nrm((N_EVEN, CONV_WIDTH, B_WIDTH), CONV_WIDTH ** -0.5),
        "conv_b": nrm((N_EVEN, B_WIDTH), 0.01),
        "rg_wa": nrm((N_EVEN, B_HEADS, B_HEAD_DIM, B_HEAD_DIM), B_HEAD_DIM ** -0.5),
        "rg_ba": nrm((N_EVEN, B_HEADS, B_HEAD_DIM), 0.01),
        "rg_wx": nrm((N_EVEN, B_HEADS, B_HEAD_DIM, B_HEAD_DIM), B_HEAD_DIM ** -0.5),
        "rg_bx": nrm((N_EVEN, B_HEADS, B_HEAD_DIM), 0.01),
        "rg_lam": rg_lam,
        "w_out_even": nrm((N_EVEN, A_WIDTH + B_WIDTH, D_MODEL), (A_WIDTH + B_WIDTH) ** -0.5),
        "ln_ffn_even": gain((N_EVEN, D_MODEL)),
        "ffn_wg": nrm((N_EVEN, D_MODEL, FFN_DIM), D_MODEL ** -0.5),
        "ffn_wu": nrm((N_EVEN, D_MODEL, FFN_DIM), D_MODEL ** -0.5),
        "ffn_wd": nrm((N_EVEN, FFN_DIM, D_MODEL), FFN_DIM ** -0.5),
        "ln_mix_odd": gain((N_ODD, D_MODEL)),
        "pool_w": nrm((N_ODD, POOL_GROUPS, POOL_GROUP_DIM, POOL_GROUP_DIM), POOL_GROUP_DIM ** -0.5),
        "pool_scale": gain((N_ODD, D_MODEL)),
        "ln_ffn_odd": gain((N_ODD, D_MODEL)),
        "router_w": nrm((N_ODD, D_MODEL, N_EXPERTS), D_MODEL ** -0.5),
        "moe_wg": nrm((N_ODD, N_EXPERTS, D_MODEL, EXPERT_DIM), D_MODEL ** -0.5),
        "moe_wu": nrm((N_ODD, N_EXPERTS, D_MODEL, EXPERT_DIM), D_MODEL ** -0.5),
        "moe_wd": nrm((N_ODD, N_EXPERTS, EXPERT_DIM, D_MODEL), EXPERT_DIM ** -0.5),
        "final_g": gain((D_MODEL,)),
    }


def reference(x, ln_mix_even, w_in_even, lam_q1, lam_k1, lam_q2, lam_k2, subln_g,
              conv_w, conv_b, rg_wa, rg_ba, rg_wx, rg_bx, rg_lam, w_out_even,
              ln_ffn_even, ffn_wg, ffn_wu, ffn_wd, ln_mix_odd, pool_w, pool_scale,
              ln_ffn_odd, router_w, moe_wg, moe_wu, moe_wd, final_g):
    cos, sin = rope_tables(x.shape[1])
    h = x
    for layer in range(DEPTH):
        j = layer // 2
        if layer % 2 == 0:
            lam_init = 0.8 - 0.6 * math.exp(-0.3 * layer)
            h = h + even_mixer(rms_norm(h, ln_mix_even[j]), w_in_even[j],
                               lam_q1[j], lam_k1[j], lam_q2[j], lam_k2[j], subln_g[j],
                               conv_w[j], conv_b[j], rg_wa[j], rg_ba[j], rg_wx[j], rg_bx[j],
                               rg_lam[j], w_out_even[j], lam_init, cos, sin)
            h = h + swiglu(rms_norm(h, ln_ffn_even[j]), ffn_wg[j], ffn_wu[j], ffn_wd[j])
        else:
            h = h + pool_mixer(rms_norm(h, ln_mix_odd[j]), pool_w[j], pool_scale[j])
            h = h + moe_swiglu(rms_norm(h, ln_ffn_odd[j]), router_w[j],
                               moe_wg[j], moe_wu[j], moe_wd[j])
    return rms_norm(h, final_g)
```

```python
import functools
import math

import jax
import jax.numpy as jnp
from jax import lax
from jax.experimental import pallas as pl
from jax.experimental.pallas import tpu as pltpu

F32 = jnp.float32
BF16 = jnp.bfloat16

D_MODEL = 1024
A_QK_DIM = 64
A_V_DIM = 128
A_WIDTH = 512
A_HEADS = 4
ROPE_DIM = 16
ROPE_THETA = 500000.0
B_WIDTH = 512
B_HEADS = 8
B_HEAD_DIM = 64
CONV_WIDTH = 4
LRU_C = 8.0
IN_WIDTH = 2560
POOL_WINDOWS = (2, 4, 8, 16)
POOL_GROUP_DIM = 256
FFN_DIM = 2816
N_EXPERTS = 8
EXPERT_DIM = 3584
NORM_EPS = 1e-6
SUBLN_EPS = 1e-5

LANES = 128
VMEM_LIMIT = 56 * 1024 * 1024
NEG_BIG = -1e30

TM_PROJ = 512
TQ = 256
TS_LRU = 512
TM_FFN = 512
TF_FFN = 1408
TS_POOL = 512
POOL_HALO = 16
TM_EXP = 1024
TF_EXP = 512
TD_DISPATCH = 1024
TC_COMBINE = 512


def _params(*sem):
    return pltpu.CompilerParams(dimension_semantics=sem, vmem_limit_bytes=VMEM_LIMIT)


def _rms(x, g, eps):
    return x * lax.rsqrt(jnp.mean(x * x, axis=-1, keepdims=True) + eps) * g


def _sigmoid(x):
    return 1.0 / (1.0 + jnp.exp(-x))


def _inproj_kernel(x_ref, g_ref, w_ref, cs_ref, s1_ref, s2_ref, o_ref):
    xn = _rms(x_ref[...], g_ref[...], NORM_EPS).astype(BF16)
    cs, s1, s2 = cs_ref[...], s1_ref[...], s2_ref[...]
    half = ROPE_DIM // 2
    for c in range(IN_WIDTH // A_WIDTH):
        y = jnp.dot(xn, w_ref[:, c * A_WIDTH:(c + 1) * A_WIDTH], preferred_element_type=F32)
        if c < 2:
            scale = A_QK_DIM ** -0.5 if c == 0 else 1.0
            for b in range(A_WIDTH // LANES):
                yb = y[:, b * LANES:(b + 1) * LANES]
                rot = (yb * cs + pltpu.roll(yb, LANES - half, 1) * s1
                       + pltpu.roll(yb, half, 1) * s2)
                o_ref[:, c * A_WIDTH + b * LANES:c * A_WIDTH + (b + 1) * LANES] = (
                    rot * scale).astype(BF16)
        else:
            o_ref[:, c * A_WIDTH:(c + 1) * A_WIDTH] = y.astype(BF16)


def _in_proj(x2, g, w_bf, cs, s1, s2, seq):
    n = x2.shape[0]
    nblk_seq = seq // TM_PROJ
    tab = pl.BlockSpec((TM_PROJ, LANES), lambda i: (i % nblk_seq, 0))
    return pl.pallas_call(
        _inproj_kernel,
        grid=(n // TM_PROJ,),
        in_specs=[
            pl.BlockSpec((TM_PROJ, D_MODEL), lambda i: (i, 0)),
            pl.BlockSpec((1, D_MODEL), lambda i: (0, 0)),
            pl.BlockSpec((D_MODEL, IN_WIDTH), lambda i: (0, 0)),
            tab, tab, tab,
        ],
        out_specs=pl.BlockSpec((TM_PROJ, IN_WIDTH), lambda i: (i, 0)),
        out_shape=jax.ShapeDtypeStruct((n, IN_WIDTH), BF16),
        compiler_params=_params("arbitrary"),
        name="in_proj",
    )(x2, g, w_bf, cs, s1, s2)


def _attn_kernel(q_ref, k_ref, v_ref, lq1_ref, lk1_ref, lq2_ref, lk2_ref, sg_ref, o_ref, *,
                 lam_init):
    qi = pl.program_id(1)
    lam = (jnp.exp(jnp.sum(lq1_ref[...] * lk1_ref[...], axis=-1, keepdims=True))
           - jnp.exp(jnp.sum(lq2_ref[...] * lk2_ref[...], axis=-1, keepdims=True))
           + lam_init)
    lane = lax.broadcasted_iota(jnp.int32, (TQ, LANES), 1)
    row = lax.broadcasted_iota(jnp.int32, (2 * TQ, TQ), 0)
    col = lax.broadcasted_iota(jnp.int32, (2 * TQ, TQ), 1)
    causal = col <= jnp.where(row >= TQ, row - TQ, row)
    for h in range(A_HEADS):
        hs = slice(h * A_V_DIM, (h + 1) * A_V_DIM)
        q = q_ref[:, hs]
        zero = jnp.zeros_like(q)
        qq = jnp.concatenate([jnp.where(lane < A_QK_DIM, q, zero),
                              jnp.where(lane >= A_QK_DIM, q, zero)], axis=0)

        def step(kv, carry, masked, hs=hs, qq=qq):
            m, l, acc = carry
            start = pl.multiple_of(kv * TQ, TQ)
            kb = k_ref[pl.ds(start, TQ), hs]
            vb = v_ref[pl.ds(start, TQ), hs]
            s = lax.dot_general(qq, kb, (((1,), (1,)), ((), ())), preferred_element_type=F32)
            if masked:
                s = jnp.where(causal, s, NEG_BIG)
            m_new = jnp.maximum(m, jnp.max(s, axis=-1, keepdims=True))
            alpha = jnp.exp(m - m_new)
            p = jnp.exp(s - m_new)
            l = alpha * l + jnp.sum(p, axis=-1, keepdims=True)
            acc = alpha * acc + jnp.dot(p.astype(BF16), vb, preferred_element_type=F32)
            return m_new, l, acc

        init = (jnp.full((2 * TQ, 1), NEG_BIG, F32), jnp.zeros((2 * TQ, 1), F32),
                jnp.zeros((2 * TQ, A_V_DIM), F32))
        carry = lax.fori_loop(0, qi, functools.partial(step, masked=False), init)
        _, l, acc = step(qi, carry, masked=True)
        o = acc / l
        o = o[:TQ] - lam * o[TQ:]
        o = _rms(o, sg_ref[...], SUBLN_EPS) * (1.0 - lam_init)
        o_ref[:, hs] = o.astype(BF16)


def _attention(proj, lq1, lk1, lq2, lk2, sg, bsz, seq, lam_init):
    n = proj.shape[0]
    nq = seq // TQ
    vec = pl.BlockSpec((1, A_QK_DIM), lambda b, i: (0, 0))
    return pl.pallas_call(
        functools.partial(_attn_kernel, lam_init=lam_init),
        grid=(bsz, nq),
        in_specs=[
            pl.BlockSpec((TQ, A_WIDTH), lambda b, i: (b * nq + i, 0)),
            pl.BlockSpec((seq, A_WIDTH), lambda b, i: (b, 1)),
            pl.BlockSpec((seq, A_WIDTH), lambda b, i: (b, 2)),
            vec, vec, vec, vec,
            pl.BlockSpec((1, A_V_DIM), lambda b, i: (0, 0)),
        ],
        out_specs=pl.BlockSpec((TQ, A_WIDTH), lambda b, i: (b * nq + i, 0)),
        out_shape=jax.ShapeDtypeStruct((n, A_WIDTH), BF16),
        compiler_params=_params("arbitrary", "arbitrary"),
        name="diff_attn",
    )(proj, proj, proj, lq1, lk1, lq2, lk2, sg)


def _rglru_kernel(xb_ref, gb_ref, cw_ref, cb_ref, wg_ref, bg_ref, lam_ref, o_ref,
                  xbuf, hprev):
    si = pl.program_id(1)
    ts = TS_LRU

    @pl.when(si == 0)
    def _():
        xbuf[0:8, :] = jnp.zeros((8, B_WIDTH), F32)
        hprev[...] = jnp.zeros_like(hprev)

    x = xb_ref[...].astype(F32)
    xbuf[8:8 + ts, :] = x
    conv = x * cw_ref[CONV_WIDTH - 1:CONV_WIDTH, :] + cb_ref[...]
    for d in range(1, CONV_WIDTH):
        conv = conv + xbuf[8 - d:8 - d + ts, :] * cw_ref[CONV_WIDTH - 1 - d:CONV_WIDTH - d, :]
    xbuf[0:8, :] = x[ts - 8:ts, :]

    gates = jnp.dot(conv.astype(BF16), wg_ref[...], preferred_element_type=F32) + bg_ref[...]
    r = _sigmoid(gates[:, :B_WIDTH])
    i = _sigmoid(gates[:, B_WIDTH:])
    z = -lam_ref[...]
    softplus = jnp.maximum(z, 0.0) + jnp.log(1.0 + jnp.exp(-jnp.abs(z)))
    a = jnp.exp(-LRU_C * r * softplus)
    u = jnp.sqrt(jnp.maximum(1.0 - a * a, 0.0)) * (i * conv)

    rows = lax.broadcasted_iota(jnp.int32, (ts, B_WIDTH), 0)
    d = 1
    while d < ts:
        keep = rows >= d
        a_sh = jnp.where(keep, pltpu.roll(a, d, 0), 1.0)
        u_sh = jnp.where(keep, pltpu.roll(u, d, 0), 0.0)
        u = a * u_sh + u
        a = a * a_sh
        d *= 2
    h = u + a * hprev[...]
    hprev[...] = h[ts - 1:ts, :]

    g = gb_ref[...].astype(F32)
    gelu = 0.5 * g * (1.0 + jnp.tanh(math.sqrt(2.0 / math.pi) * (g + 0.044715 * (g * g * g))))
    o_ref[...] = (h * gelu).astype(BF16)


def _rglru(proj, conv_w, conv_b, w_gates, b_gates, lam, bsz, seq):
    n = proj.shape[0]
    ns = seq // TS_LRU
    row = lambda b, s: b * ns + s
    const = lambda b, s: (0, 0)
    return pl.pallas_call(
        _rglru_kernel,
        grid=(bsz, ns),
        in_specs=[
            pl.BlockSpec((TS_LRU, B_WIDTH), lambda b, s: (row(b, s), 3)),
            pl.BlockSpec((TS_LRU, B_WIDTH), lambda b, s: (row(b, s), 4)),
            pl.BlockSpec((CONV_WIDTH, B_WIDTH), const),
            pl.BlockSpec((1, B_WIDTH), const),
            pl.BlockSpec((B_WIDTH, 2 * B_WIDTH), const),
            pl.BlockSpec((1, 2 * B_WIDTH), const),
            pl.BlockSpec((1, B_WIDTH), const),
        ],
        out_specs=pl.BlockSpec((TS_LRU, B_WIDTH), lambda b, s: (row(b, s), 0)),
        out_shape=jax.ShapeDtypeStruct((n, B_WIDTH), BF16),
        scratch_shapes=[pltpu.VMEM((8 + TS_LRU, B_WIDTH), F32), pltpu.VMEM((1, B_WIDTH), F32)],
        compiler_params=_params("arbitrary", "arbitrary"),
        name="rglru",
    )(proj, proj, conv_w, conv_b, w_gates, b_gates, lam)


def _ffn_kernel(x_ref, at_ref, rc_ref, wo_ref, g_ref, wg_ref, wu_ref, wd_ref, o_ref, hn, acc):
    j = pl.program_id(1)

    @pl.when(j == 0)
    def _():
        h = (x_ref[...]
             + jnp.dot(at_ref[...], wo_ref[0:A_WIDTH, :], preferred_element_type=F32)
             + jnp.dot(rc_ref[...], wo_ref[A_WIDTH:, :], preferred_element_type=F32))
        hn[...] = _rms(h, g_ref[...], NORM_EPS).astype(BF16)
        acc[...] = h

    x = hn[...]
    gate = jnp.dot(x, wg_ref[...], preferred_element_type=F32)
    up = jnp.dot(x, wu_ref[...], preferred_element_type=F32)
    mid = (gate * _sigmoid(gate) * up).astype(BF16)
    acc[...] += jnp.dot(mid, wd_ref[...], preferred_element_type=F32)

    @pl.when(j == pl.num_programs(1) - 1)
    def _():
        o_ref[...] = acc[...]


def _ffn(x2, attn, rec, wo, g, wg, wu, wd):
    n = x2.shape[0]
    const = lambda i, j: (0, 0)
    rows = lambda i, j: (i, 0)
    return pl.pallas_call(
        _ffn_kernel,
        grid=(n // TM_FFN, FFN_DIM // TF_FFN),
        in_specs=[
            pl.BlockSpec((TM_FFN, D_MODEL), rows),
            pl.BlockSpec((TM_FFN, A_WIDTH), rows),
            pl.BlockSpec((TM_FFN, B_WIDTH), rows),
            pl.BlockSpec((D_MODEL, D_MODEL), const),
            pl.BlockSpec((1, D_MODEL), const),
            pl.BlockSpec((D_MODEL, TF_FFN), lambda i, j: (0, j)),
            pl.BlockSpec((D_MODEL, TF_FFN), lambda i, j: (0, j)),
            pl.BlockSpec((TF_FFN, D_MODEL), lambda i, j: (j, 0)),
        ],
        out_specs=pl.BlockSpec((TM_FFN, D_MODEL), rows),
        out_shape=jax.ShapeDtypeStruct((n, D_MODEL), F32),
        scratch_shapes=[pltpu.VMEM((TM_FFN, D_MODEL), BF16), pltpu.VMEM((TM_FFN, D_MODEL), F32)],
        compiler_params=_params("arbitrary", "arbitrary"),
        name="outproj_ffn",
    )(x2, attn, rec, wo, g, wg, wu, wd)


def _pool_router_kernel(h_ref, g1_ref, pw_ref, ps_ref, g2_ref, rw_ref,
                        h2_ref, mi_ref, mf_ref, cnt_ref, xbuf, carry):
    b = pl.program_id(0)
    si = pl.program_id(1)
    ts = TS_POOL

    @pl.when(si == 0)
    def _():
        xbuf[0:POOL_HALO, :] = jnp.zeros((POOL_HALO, D_MODEL), F32)

    @pl.when((si == 0) & (b == 0))
    def _():
        carry[...] = jnp.zeros_like(carry)

    h = h_ref[...]
    hn = _rms(h, g1_ref[...], NORM_EPS)
    xbuf[POOL_HALO:POOL_HALO + ts, :] = hn
    pos = si * ts + lax.broadcasted_iota(jnp.int32, (ts, 1), 0)
    for gi, w in enumerate(POOL_WINDOWS):
        cs = slice(gi * POOL_GROUP_DIM, (gi + 1) * POOL_GROUP_DIM)
        xg = hn[:, cs]
        tot = xg
        for j in range(1, w):
            tot = tot + xbuf[POOL_HALO - j:POOL_HALO - j + ts, cs]
        count = jnp.minimum(pos + 1, w).astype(F32)
        dg = (tot / count - xg).astype(BF16)
        y = jnp.dot(dg, pw_ref[gi], preferred_element_type=F32) * ps_ref[:, cs]
        h2_ref[:, cs] = h[:, cs] + y
    xbuf[0:POOL_HALO, :] = hn[ts - POOL_HALO:ts, :]

    h2 = h2_ref[...]
    hn2 = _rms(h2, g2_ref[...], NORM_EPS)
    logits = jnp.dot(hn2, rw_ref[...], preferred_element_type=F32,
                     precision=lax.Precision.HIGHEST)
    lane = lax.broadcasted_iota(jnp.int32, (ts, LANES), 1)
    lane_f = lane.astype(F32)
    lg = jnp.where(lane < N_EXPERTS, logits, -jnp.inf)
    m1 = jnp.max(lg, axis=-1, keepdims=True)
    i1 = jnp.min(jnp.where(lg == m1, lane_f, float(LANES)), axis=-1, keepdims=True)
    lg2 = jnp.where(lane_f == i1, -jnp.inf, lg)
    m2 = jnp.max(lg2, axis=-1, keepdims=True)
    i2 = jnp.min(jnp.where(lg2 == m2, lane_f, float(LANES)), axis=-1, keepdims=True)
    e = jnp.exp(m2 - m1)
    gate1 = 1.0 / (1.0 + e)
    gate2 = e / (1.0 + e)

    sel1 = lane_f == i1
    sel2 = lane_f == i2
    onehot = jnp.where(sel1, 1.0, jnp.where(sel2, 1.0, 0.0)).astype(BF16)
    tr = lax.broadcasted_iota(jnp.int32, (ts, ts), 0)
    tc = lax.broadcasted_iota(jnp.int32, (ts, ts), 1)
    lower = jnp.where(tc < tr, 1.0, 0.0).astype(BF16)
    before = jnp.dot(lower, onehot, preferred_element_type=F32) + carry[...]
    rank1 = jnp.sum(jnp.where(sel1, before, 0.0), axis=-1, keepdims=True)
    rank2 = jnp.sum(jnp.where(sel2, before, 0.0), axis=-1, keepdims=True)
    carry[...] = carry[...] + jnp.sum(onehot.astype(F32), axis=0, keepdims=True)
    cnt_ref[...] = jnp.broadcast_to(carry[...], cnt_ref.shape).astype(jnp.int32)

    r1 = rank1.astype(jnp.int32)
    r2 = rank2.astype(jnp.int32)
    mi_ref[...] = jnp.where(lane == 0, i1.astype(jnp.int32),
                            jnp.where(lane == 1, i2.astype(jnp.int32),
                                      jnp.where(lane == 2, r1, jnp.where(lane == 3, r2, 0))))
    mf_ref[...] = jnp.where(lane == 0, gate1, jnp.where(lane == 1, gate2, 0.0))


def _pool_router(h1, g1, pw, ps, g2, rw, bsz, seq):
    n = h1.shape[0]
    ns = seq // TS_POOL
    row = lambda b, s: (b * ns + s, 0)
    const = lambda b, s: (0, 0)
    return pl.pallas_call(
        _pool_router_kernel,
        grid=(bsz, ns),
        in_specs=[
            pl.BlockSpec((TS_POOL, D_MODEL), row),
            pl.BlockSpec((1, D_MODEL), const),
            pl.BlockSpec((len(POOL_WINDOWS), POOL_GROUP_DIM, POOL_GROUP_DIM),
                         lambda b, s: (0, 0, 0)),
            pl.BlockSpec((1, D_MODEL), const),
            pl.BlockSpec((1, D_MODEL), const),
            pl.BlockSpec((D_MODEL, LANES), const),
        ],
        out_specs=[
            pl.BlockSpec((TS_POOL, D_MODEL), row),
            pl.BlockSpec((TS_POOL, LANES), row),
            pl.BlockSpec((TS_POOL, LANES), row),
            pl.BlockSpec((8, LANES), const),
        ],
        out_shape=[
            jax.ShapeDtypeStruct((n, D_MODEL), F32),
            jax.ShapeDtypeStruct((n, LANES), jnp.int32),
            jax.ShapeDtypeStruct((n, LANES), F32),
            jax.ShapeDtypeStruct((8, LANES), jnp.int32),
        ],
        scratch_shapes=[pltpu.VMEM((POOL_HALO + TS_POOL, D_MODEL), F32),
                        pltpu.VMEM((1, LANES), F32)],
        compiler_params=_params("arbitrary", "arbitrary"),
        name="pool_router",
    )(h1, g1, pw, ps, g2, rw)


def _dispatch_kernel(pos_ref, fill_ref, h_hbm, xs_hbm, zbuf, sem, fsem):
    i = pl.program_id(0)
    td = TD_DISPATCH
    base = i * td

    def issue(r, c):
        src = h_hbm.at[pl.ds(base + r, 1)]
        pltpu.make_async_copy(src, xs_hbm.at[pl.ds(pos_ref[0, 0, r], 1)], sem).start()
        pltpu.make_async_copy(src, xs_hbm.at[pl.ds(pos_ref[0, 0, td + r], 1)], sem).start()
        return c

    lax.fori_loop(0, td, issue, 0)

    @pl.when(i == pl.num_programs(0) - 1)
    def _():
        zbuf[...] = jnp.zeros_like(zbuf)
        zrow = zbuf.at[pl.ds(0, 1)]

        def row_copy(r):
            return pltpu.make_async_copy(zrow, xs_hbm.at[pl.ds(r, 1)], fsem)

        def tile_copy(t):
            start = pl.multiple_of(t * TM_EXP, TM_EXP)
            return pltpu.make_async_copy(zbuf, xs_hbm.at[pl.ds(start, TM_EXP)], fsem)

        def run(lo, hi, copy):
            def start(r, c):
                copy(r).start()
                return c

            def wait(r, c):
                copy(r).wait()
                return c

            lax.fori_loop(lo, hi, start, 0)
            lax.fori_loop(lo, hi, wait, 0)

        for e in range(N_EXPERTS):
            run(fill_ref[0, e], fill_ref[1, e], row_copy)
        run(fill_ref[2, 0], xs_hbm.shape[0] // TM_EXP, tile_copy)

    blk = h_hbm.at[pl.ds(0, td)]
    pltpu.make_async_copy(blk, xs_hbm.at[pl.ds(0, td)], sem).wait()
    pltpu.make_async_copy(blk, xs_hbm.at[pl.ds(0, td)], sem).wait()


def _dispatch(h2, pos3, fill, rows_padded):
    n = h2.shape[0]
    return pl.pallas_call(
        _dispatch_kernel,
        grid=(n // TD_DISPATCH,),
        in_specs=[
            pl.BlockSpec((1, 1, 2 * TD_DISPATCH), lambda i: (i, 0, 0), memory_space=pltpu.SMEM),
            pl.BlockSpec(memory_space=pltpu.SMEM),
            pl.BlockSpec(memory_space=pl.ANY),
        ],
        out_specs=pl.BlockSpec(memory_space=pl.ANY),
        out_shape=jax.ShapeDtypeStruct((rows_padded, D_MODEL), F32),
        scratch_shapes=[pltpu.VMEM((TM_EXP, D_MODEL), F32), pltpu.SemaphoreType.DMA(()),
                        pltpu.SemaphoreType.DMA(())],
        compiler_params=_params("arbitrary"),
        name="dispatch",
    )(pos3, fill, h2)


def _experts_kernel(te_ref, tr_ref, na_ref, x_ref, g_ref, wg_ref, wu_ref, wd_ref, y_ref,
                    xn, acc):
    i = pl.program_id(0)
    j = pl.program_id(1)

    @pl.when(i < na_ref[0])
    def _():
        @pl.when(j == 0)
        def _():
            xn[...] = _rms(x_ref[...], g_ref[...], NORM_EPS).astype(BF16)
            acc[...] = jnp.zeros_like(acc)

        x = xn[...]
        gate = jnp.dot(x, wg_ref[...], preferred_element_type=F32)
        up = jnp.dot(x, wu_ref[...], preferred_element_type=F32)
        mid = (gate * _sigmoid(gate) * up).astype(BF16)
        acc[...] += jnp.dot(mid, wd_ref[...], preferred_element_type=F32)

        @pl.when(j == pl.num_programs(1) - 1)
        def _():
            y_ref[...] = acc[...]

    @pl.when((i >= na_ref[0]) & (j == 0))
    def _():
        y_ref[...] = jnp.zeros_like(y_ref)


def _experts(tile_expert, tile_row, n_active, xs, g, wg, wu, wd):
    rows = xs.shape[0]
    ntiles = rows // TM_EXP
    nf = EXPERT_DIM // TF_EXP

    def fcol(i, j, na):
        return jnp.where(i < na[0], j, nf - 1)

    grid_spec = pltpu.PrefetchScalarGridSpec(
        num_scalar_prefetch=3,
        grid=(ntiles, nf),
        in_specs=[
            pl.BlockSpec((TM_EXP, D_MODEL), lambda i, j, te, tr, na: (tr[i], 0)),
            pl.BlockSpec((1, D_MODEL), lambda i, j, te, tr, na: (0, 0)),
            pl.BlockSpec((None, D_MODEL, TF_EXP), lambda i, j, te, tr, na: (te[i], 0, fcol(i, j, na))),
            pl.BlockSpec((None, D_MODEL, TF_EXP), lambda i, j, te, tr, na: (te[i], 0, fcol(i, j, na))),
            pl.BlockSpec((None, TF_EXP, D_MODEL), lambda i, j, te, tr, na: (te[i], fcol(i, j, na), 0)),
        ],
        out_specs=pl.BlockSpec((TM_EXP, D_MODEL), lambda i, j, te, tr, na: (i, 0)),
        scratch_shapes=[pltpu.VMEM((TM_EXP, D_MODEL), BF16), pltpu.VMEM((TM_EXP, D_MODEL), F32)],
    )
    return pl.pallas_call(
        _experts_kernel,
        grid_spec=grid_spec,
        out_shape=jax.ShapeDtypeStruct((rows, D_MODEL), F32),
        compiler_params=_params("arbitrary", "arbitrary"),
        name="experts",
    )(tile_expert, tile_row, n_active, xs, g, wg, wu, wd)


def _combine_kernel(pos_ref, h_ref, mf_ref, g_ref, y_hbm, o_ref, ybuf, sem):
    tc = TC_COMBINE

    def issue(r, c):
        pltpu.make_async_copy(y_hbm.at[pl.ds(pos_ref[0, 0, r], 1)],
                              ybuf.at[0, pl.ds(r, 1)], sem).start()
        pltpu.make_async_copy(y_hbm.at[pl.ds(pos_ref[0, 0, tc + r], 1)],
                              ybuf.at[1, pl.ds(r, 1)], sem).start()
        return c

    lax.fori_loop(0, tc, issue, 0)
    blk = y_hbm.at[pl.ds(0, tc)]
    pltpu.make_async_copy(blk, ybuf.at[0], sem).wait()
    pltpu.make_async_copy(blk, ybuf.at[1], sem).wait()

    mf = mf_ref[...]
    out = h_ref[...] + mf[:, 0:1] * ybuf[0] + mf[:, 1:2] * ybuf[1]
    o_ref[...] = _rms(out, g_ref[...], NORM_EPS)


def _combine(pos3, h2, mf, g, y):
    n = h2.shape[0]
    return pl.pallas_call(
        _combine_kernel,
        grid=(n // TC_COMBINE,),
        in_specs=[
            pl.BlockSpec((1, 1, 2 * TC_COMBINE), lambda i: (i, 0, 0), memory_space=pltpu.SMEM),
            pl.BlockSpec((TC_COMBINE, D_MODEL), lambda i: (i, 0)),
            pl.BlockSpec((TC_COMBINE, LANES), lambda i: (i, 0)),
            pl.BlockSpec((1, D_MODEL), lambda i: (0, 0)),
            pl.BlockSpec(memory_space=pl.ANY),
        ],
        out_specs=pl.BlockSpec((TC_COMBINE, D_MODEL), lambda i: (i, 0)),
        out_shape=jax.ShapeDtypeStruct((n, D_MODEL), F32),
        scratch_shapes=[pltpu.VMEM((2, TC_COMBINE, D_MODEL), F32), pltpu.SemaphoreType.DMA(())],
        compiler_params=_params("arbitrary"),
        name="combine",
    )(pos3, h2, mf, g, y)


def _rope_tables(seq):
    half = ROPE_DIM // 2
    inv = ROPE_THETA ** (-jnp.arange(0, ROPE_DIM, 2, dtype=F32) / ROPE_DIM)
    ang = jnp.arange(seq, dtype=F32)[:, None] * inv[None, :]
    cos, sin = jnp.cos(ang), jnp.sin(ang)
    lane = jnp.arange(LANES) % A_QK_DIM
    idx = lane % half
    cs = jnp.where(lane < ROPE_DIM, cos[:, idx], 1.0)
    s1 = jnp.where(lane < half, -sin[:, idx], 0.0)
    s2 = jnp.where((lane >= half) & (lane < ROPE_DIM), sin[:, idx], 0.0)
    return cs, s1, s2


def _block_diag(w):
    heads, hd, _ = w.shape
    eye = jnp.eye(heads, dtype=w.dtype)
    return jnp.einsum("hij,hg->higj", w, eye).reshape(heads * hd, heads * hd)


def _routing_tables(meta_i, counts, n):
    tm = TM_EXP
    ntiles = (2 * n) // tm + N_EXPERTS
    cnt = counts[0, :N_EXPERTS]
    tiles_e = (cnt + tm - 1) // tm
    cum_tiles = jnp.cumsum(tiles_e)
    off = (cum_tiles - tiles_e) * tm
    pos1 = off[meta_i[:, 0]] + meta_i[:, 2]
    pos2 = off[meta_i[:, 1]] + meta_i[:, 3]
    n_active = cum_tiles[-1]
    t = jnp.arange(ntiles, dtype=jnp.int32)
    t_eff = jnp.minimum(t, n_active - 1)
    tile_expert = jnp.minimum(jnp.searchsorted(cum_tiles, t_eff, side="right"),
                              N_EXPERTS - 1).astype(jnp.int32)
    fill = jnp.stack([off + cnt, off + tiles_e * tm,
                      jnp.broadcast_to(n_active, (N_EXPERTS,))]).astype(jnp.int32)
    return (pos1.astype(jnp.int32), pos2.astype(jnp.int32), tile_expert,
            t_eff.astype(jnp.int32), n_active.astype(jnp.int32).reshape(1), fill, ntiles * tm)


def _pos_blocks(pos1, pos2, blk):
    nb = pos1.shape[0] // blk
    return jnp.concatenate([pos1.reshape(nb, 1, blk), pos2.reshape(nb, 1, blk)], axis=-1)


def kernel(x, ln_mix_even, w_in_even, lam_q1, lam_k1, lam_q2, lam_k2, subln_g, conv_w, conv_b,
           rg_wa, rg_ba, rg_wx, rg_bx, rg_lam, w_out_even, ln_ffn_even, ffn_wg, ffn_wu, ffn_wd,
           ln_mix_odd, pool_w, pool_scale, ln_ffn_odd, router_w, moe_wg, moe_wu, moe_wd, final_g):
    bsz, seq, dm = x.shape
    n = bsz * seq
    x2 = x.reshape(n, dm)
    row = lambda v: v.reshape(1, -1)

    lam_init = 0.8 - 0.6 * math.exp(-0.3 * 0)
    cs, s1, s2 = _rope_tables(seq)
    proj = _in_proj(x2, row(ln_mix_even[0]), w_in_even[0].astype(BF16), cs, s1, s2, seq)
    attn = _attention(proj, row(lam_q1[0]), row(lam_k1[0]), row(lam_q2[0]), row(lam_k2[0]),
                      row(subln_g[0]), bsz, seq, lam_init)
    w_gates = jnp.concatenate([_block_diag(rg_wa[0]), _block_diag(rg_wx[0])], axis=1).astype(BF16)
    b_gates = jnp.concatenate([rg_ba[0].reshape(1, -1), rg_bx[0].reshape(1, -1)], axis=1)
    rec = _rglru(proj, conv_w[0], row(conv_b[0]), w_gates, b_gates, row(rg_lam[0]), bsz, seq)
    h1 = _ffn(x2, attn, rec, w_out_even[0].astype(BF16), row(ln_ffn_even[0]),
              ffn_wg[0].astype(BF16), ffn_wu[0].astype(BF16), ffn_wd[0].astype(BF16))

    rw = jnp.pad(router_w[0], ((0, 0), (0, LANES - N_EXPERTS)))
    h2, meta_i, meta_f, counts = _pool_router(
        h1, row(ln_mix_odd[0]), pool_w[0].astype(BF16), row(pool_scale[0]),
        row(ln_ffn_odd[0]), rw, bsz, seq)
    pos1, pos2, tile_expert, tile_row, n_active, fill, rows_padded = _routing_tables(
        meta_i, counts, n)
    xs = _dispatch(h2, _pos_blocks(pos1, pos2, TD_DISPATCH), fill, rows_padded)
    y = _experts(tile_expert, tile_row, n_active, xs, row(ln_ffn_odd[0]),
                 moe_wg[0].astype(BF16), moe_wu[0].astype(BF16), moe_wd[0].astype(BF16))
    out = _combine(_pos_blocks(pos1, pos2, TC_COMBINE), h2, meta_f, row(final_g), y)
    return out.reshape(bsz, seq, dm)
```

```python
import functools
import math

import jax
import jax.numpy as jnp
from jax import lax
from jax.experimental import pallas as pl
from jax.experimental.pallas import tpu as pltpu

F32 = jnp.float32
BF16 = jnp.bfloat16

D_MODEL = 1024
A_QK_DIM = 64
A_V_DIM = 128
A_WIDTH = 512
A_HEADS = 4
ROPE_DIM = 16
ROPE_THETA = 500000.0
B_WIDTH = 512
B_HEADS = 8
B_HEAD_DIM = 64
CONV_WIDTH = 4
LRU_C = 8.0
IN_WIDTH = 2560
POOL_WINDOWS = (2, 4, 8, 16)
POOL_GROUP_DIM = 256
FFN_DIM = 2816
N_EXPERTS = 8
EXPERT_DIM = 3584
NORM_EPS = 1e-6
SUBLN_EPS = 1e-5

LANES = 128
VMEM_LIMIT = 56 * 1024 * 1024
NEG_BIG = -1e30

TM_PROJ = 512
TQ = 256
TS_LRU = 512
TM_FFN = 512
TF_FFN = 1408
TS_POOL = 512
POOL_HALO = 16
TM_EXP = 1024
TF_EXP = 512
TD_DISPATCH = 1024
TC_COMBINE = 512


def _params(*sem):
    return pltpu.CompilerParams(dimension_semantics=sem, vmem_limit_bytes=VMEM_LIMIT)


def _rms(x, g, eps):
    return x * lax.rsqrt(jnp.mean(x * x, axis=-1, keepdims=True) + eps) * g


def _sigmoid(x):
    return 1.0 / (1.0 + jnp.exp(-x))


def _inproj_kernel(x_ref, g_ref, w_ref, cs_ref, s1_ref, s2_ref, o_ref):
    xn = _rms(x_ref[...], g_ref[...], NORM_EPS).astype(BF16)
    cs, s1, s2 = cs_ref[...], s1_ref[...], s2_ref[...]
    half = ROPE_DIM // 2
    for c in range(IN_WIDTH // A_WIDTH):
        y = jnp.dot(xn, w_ref[:, c * A_WIDTH:(c + 1) * A_WIDTH], preferred_element_type=F32)
        if c < 2:
            scale = A_QK_DIM ** -0.5 * math.log2(math.e) if c == 0 else 1.0
            for b in range(A_WIDTH // LANES):
                yb = y[:, b * LANES:(b + 1) * LANES]
                rot = (yb * cs + pltpu.roll(yb, LANES - half, 1) * s1
                       + pltpu.roll(yb, half, 1) * s2)
                o_ref[:, c * A_WIDTH + b * LANES:c * A_WIDTH + (b + 1) * LANES] = (
                    rot * scale).astype(BF16)
        else:
            o_ref[:, c * A_WIDTH:(c + 1) * A_WIDTH] = y.astype(BF16)


def _in_proj(x2, g, w_bf, cs, s1, s2, seq):
    n = x2.shape[0]
    nblk_seq = seq // TM_PROJ
    tab = pl.BlockSpec((TM_PROJ, LANES), lambda i: (i % nblk_seq, 0))
    return pl.pallas_call(
        _inproj_kernel,
        grid=(n // TM_PROJ,),
        in_specs=[
            pl.BlockSpec((TM_PROJ, D_MODEL), lambda i: (i, 0)),
            pl.BlockSpec((1, D_MODEL), lambda i: (0, 0)),
            pl.BlockSpec((D_MODEL, IN_WIDTH), lambda i: (0, 0)),
            tab, tab, tab,
        ],
        out_specs=pl.BlockSpec((TM_PROJ, IN_WIDTH), lambda i: (i, 0)),
        out_shape=jax.ShapeDtypeStruct((n, IN_WIDTH), BF16),
        compiler_params=_params("arbitrary"),
        name="in_proj",
    )(x2, g, w_bf, cs, s1, s2)


def _attn_kernel(q_ref, k_ref, vt_ref, lq1_ref, lk1_ref, lq2_ref, lk2_ref, sg_ref, o_ref,
                 qq, m_sc, l_sc, acc, *, lam_init):
    qi = pl.program_id(1)
    lane = lax.broadcasted_iota(jnp.int32, (TQ, LANES), 1)
    for h in range(A_HEADS):
        q = q_ref[:, h * A_V_DIM:(h + 1) * A_V_DIM]
        zero = jnp.zeros_like(q)
        qq[h, 0:TQ, :] = jnp.where(lane < A_QK_DIM, q, zero)
        qq[h, TQ:2 * TQ, :] = jnp.where(lane >= A_QK_DIM, q, zero)
    m_sc[...] = jnp.full_like(m_sc, NEG_BIG)
    l_sc[...] = jnp.zeros_like(l_sc)
    acc[...] = jnp.zeros_like(acc)

    def step(kv, masked):
        start = pl.multiple_of(kv * TQ, TQ)

        def scores(h):
            kb = k_ref[pl.ds(start, TQ), h * A_V_DIM:(h + 1) * A_V_DIM]
            return lax.dot_general(kb, qq[h], (((1,), (1,)), ((), ())),
                                   preferred_element_type=F32)

        st_next = scores(0)
        for h in range(A_HEADS):
            hs = slice(h * A_V_DIM, (h + 1) * A_V_DIM)
            st = st_next
            if h + 1 < A_HEADS:
                st_next = scores(h + 1)
            if masked:
                key = lax.broadcasted_iota(jnp.int32, (TQ, 2 * TQ), 0)
                qry = lax.broadcasted_iota(jnp.int32, (TQ, 2 * TQ), 1)
                st = jnp.where(key <= jnp.where(qry >= TQ, qry - TQ, qry), st, NEG_BIG)
            m_old = m_sc[h]
            m_new = jnp.maximum(m_old, jnp.max(st, axis=0, keepdims=True))
            alpha = jnp.exp2(m_old - m_new)
            p = jnp.exp2(st - m_new)
            l_sc[h] = alpha * l_sc[h] + jnp.sum(p, axis=0, keepdims=True)
            acc[h] = alpha * acc[h] + jnp.dot(vt_ref[kv, hs, :], p.astype(BF16),
                                              preferred_element_type=F32)
            m_sc[h] = m_new

    def body(kv, c):
        step(kv, masked=False)
        return c

    lax.fori_loop(0, qi, body, 0)
    step(qi, masked=True)

    lam = (jnp.exp(jnp.sum(lq1_ref[...] * lk1_ref[...], axis=-1, keepdims=True))
           - jnp.exp(jnp.sum(lq2_ref[...] * lk2_ref[...], axis=-1, keepdims=True))
           + lam_init)
    for h in range(A_HEADS):
        o = acc[h] / l_sc[h]
        o = o[:, :TQ] - lam * o[:, TQ:]
        o = o * lax.rsqrt(jnp.mean(o * o, axis=0, keepdims=True) + SUBLN_EPS) * sg_ref[...]
        o_ref[:, h * A_V_DIM:(h + 1) * A_V_DIM] = (o * (1.0 - lam_init)).T.astype(BF16)


def _attention(proj, vt, lq1, lk1, lq2, lk2, sg, bsz, seq, lam_init):
    n = proj.shape[0]
    nq = seq // TQ
    vec = pl.BlockSpec((1, A_QK_DIM), lambda b, i: (0, 0))
    return pl.pallas_call(
        functools.partial(_attn_kernel, lam_init=lam_init),
        grid=(bsz, nq),
        in_specs=[
            pl.BlockSpec((TQ, A_WIDTH), lambda b, i: (b * nq + i, 0)),
            pl.BlockSpec((seq, A_WIDTH), lambda b, i: (b, 1)),
            pl.BlockSpec((None, nq, A_WIDTH, TQ), lambda b, i: (b, 0, 0, 0)),
            vec, vec, vec, vec,
            pl.BlockSpec((A_V_DIM, 1), lambda b, i: (0, 0)),
        ],
        out_specs=pl.BlockSpec((TQ, A_WIDTH), lambda b, i: (b * nq + i, 0)),
        out_shape=jax.ShapeDtypeStruct((n, A_WIDTH), BF16),
        scratch_shapes=[pltpu.VMEM((A_HEADS, 2 * TQ, A_V_DIM), BF16),
                        pltpu.VMEM((A_HEADS, 1, 2 * TQ), F32),
                        pltpu.VMEM((A_HEADS, 1, 2 * TQ), F32),
                        pltpu.VMEM((A_HEADS, A_V_DIM, 2 * TQ), F32)],
        compiler_params=_params("arbitrary", "arbitrary"),
        name="diff_attn",
    )(proj, proj, vt, lq1, lk1, lq2, lk2, sg)


def _rglru_kernel(xb_ref, gb_ref, cw_ref, cb_ref, wg_ref, bg_ref, lam_ref, o_ref,
                  xbuf, hprev):
    si = pl.program_id(1)
    ts = TS_LRU

    @pl.when(si == 0)
    def _():
        xbuf[0:8, :] = jnp.zeros((8, B_WIDTH), F32)
        hprev[...] = jnp.zeros_like(hprev)

    x = xb_ref[...].astype(F32)
    xbuf[8:8 + ts, :] = x
    conv = x * cw_ref[CONV_WIDTH - 1:CONV_WIDTH, :] + cb_ref[...]
    for d in range(1, CONV_WIDTH):
        conv = conv + xbuf[8 - d:8 - d + ts, :] * cw_ref[CONV_WIDTH - 1 - d:CONV_WIDTH - d, :]
    xbuf[0:8, :] = x[ts - 8:ts, :]

    gates = jnp.dot(conv.astype(BF16), wg_ref[...], preferred_element_type=F32) + bg_ref[...]
    r = _sigmoid(gates[:, :B_WIDTH])
    i = _sigmoid(gates[:, B_WIDTH:])
    z = -lam_ref[...]
    softplus = jnp.maximum(z, 0.0) + jnp.log(1.0 + jnp.exp(-jnp.abs(z)))
    a = jnp.exp(-LRU_C * r * softplus)
    u = jnp.sqrt(jnp.maximum(1.0 - a * a, 0.0)) * (i * conv)

    rows = lax.broadcasted_iota(jnp.int32, (ts, B_WIDTH), 0)
    d = 1
    while d < ts:
        keep = rows >= d
        a_sh = jnp.where(keep, pltpu.roll(a, d, 0), 1.0)
        u_sh = jnp.where(keep, pltpu.roll(u, d, 0), 0.0)
        u = a * u_sh + u
        a = a * a_sh
        d *= 2
    h = u + a * hprev[...]
    hprev[...] = h[ts - 1:ts, :]

    g = gb_ref[...].astype(F32)
    gelu = 0.5 * g * (1.0 + jnp.tanh(math.sqrt(2.0 / math.pi) * (g + 0.044715 * (g * g * g))))
    o_ref[...] = (h * gelu).astype(BF16)


def _rglru(proj, conv_w, conv_b, w_gates, b_gates, lam, bsz, seq):
    n = proj.shape[0]
    ns = seq // TS_LRU
    row = lambda b, s: b * ns + s
    const = lambda b, s: (0, 0)
    return pl.pallas_call(
        _rglru_kernel,
        grid=(bsz, ns),
        in_specs=[
            pl.BlockSpec((TS_LRU, B_WIDTH), lambda b, s: (row(b, s), 3)),
            pl.BlockSpec((TS_LRU, B_WIDTH), lambda b, s: (row(b, s), 4)),
            pl.BlockSpec((CONV_WIDTH, B_WIDTH), const),
            pl.BlockSpec((1, B_WIDTH), const),
            pl.BlockSpec((B_WIDTH, 2 * B_WIDTH), const),
            pl.BlockSpec((1, 2 * B_WIDTH), const),
            pl.BlockSpec((1, B_WIDTH), const),
        ],
        out_specs=pl.BlockSpec((TS_LRU, B_WIDTH), lambda b, s: (row(b, s), 0)),
        out_shape=jax.ShapeDtypeStruct((n, B_WIDTH), BF16),
        scratch_shapes=[pltpu.VMEM((8 + TS_LRU, B_WIDTH), F32), pltpu.VMEM((1, B_WIDTH), F32)],
        compiler_params=_params("arbitrary", "arbitrary"),
        name="rglru",
    )(proj, proj, conv_w, conv_b, w_gates, b_gates, lam)


def _ffn_kernel(x_ref, at_ref, rc_ref, wo_ref, g_ref, wg_ref, wu_ref, wd_ref, o_ref, hn, acc):
    j = pl.program_id(1)

    @pl.when(j == 0)
    def _():
        h = (x_ref[...]
             + jnp.dot(at_ref[...], wo_ref[0:A_WIDTH, :], preferred_element_type=F32)
             + jnp.dot(rc_ref[...], wo_ref[A_WIDTH:, :], preferred_element_type=F32))
        hn[...] = _rms(h, g_ref[...], NORM_EPS).astype(BF16)
        acc[...] = h

    x = hn[...]
    gate = jnp.dot(x, wg_ref[...], preferred_element_type=F32)
    up = jnp.dot(x, wu_ref[...], preferred_element_type=F32)
    mid = (gate * _sigmoid(gate) * up).astype(BF16)
    acc[...] += jnp.dot(mid, wd_ref[...], preferred_element_type=F32)

    @pl.when(j == pl.num_programs(1) - 1)
    def _():
        o_ref[...] = acc[...]


def _ffn(x2, attn, rec, wo, g, wg, wu, wd):
    n = x2.shape[0]
    const = lambda i, j: (0, 0)
    rows = lambda i, j: (i, 0)
    return pl.pallas_call(
        _ffn_kernel,
        grid=(n // TM_FFN, FFN_DIM // TF_FFN),
        in_specs=[
            pl.BlockSpec((TM_FFN, D_MODEL), rows),
            pl.BlockSpec((TM_FFN, A_WIDTH), rows),
            pl.BlockSpec((TM_FFN, B_WIDTH), rows),
            pl.BlockSpec((D_MODEL, D_MODEL), const),
            pl.BlockSpec((1, D_MODEL), const),
            pl.BlockSpec((D_MODEL, TF_FFN), lambda i, j: (0, j)),
            pl.BlockSpec((D_MODEL, TF_FFN), lambda i, j: (0, j)),
            pl.BlockSpec((TF_FFN, D_MODEL), lambda i, j: (j, 0)),
        ],
        out_specs=pl.BlockSpec((TM_FFN, D_MODEL), rows),
        out_shape=jax.ShapeDtypeStruct((n, D_MODEL), F32),
        scratch_shapes=[pltpu.VMEM((TM_FFN, D_MODEL), BF16), pltpu.VMEM((TM_FFN, D_MODEL), F32)],
        compiler_params=_params("arbitrary", "arbitrary"),
        name="outproj_ffn",
    )(x2, attn, rec, wo, g, wg, wu, wd)


def _pool_router_kernel(h_ref, g1_ref, pw_ref, ps_ref, g2_ref, rw_ref,
                        h2_ref, mi_ref, mf_ref, cnt_ref, xbuf, carry):
    b = pl.program_id(0)
    si = pl.program_id(1)
    ts = TS_POOL

    @pl.when(si == 0)
    def _():
        xbuf[0:POOL_HALO, :] = jnp.zeros((POOL_HALO, D_MODEL), F32)

    @pl.when((si == 0) & (b == 0))
    def _():
        carry[...] = jnp.zeros_like(carry)

    h = h_ref[...]
    hn = _rms(h, g1_ref[...], NORM_EPS)
    xbuf[POOL_HALO:POOL_HALO + ts, :] = hn
    pos = si * ts + lax.broadcasted_iota(jnp.int32, (ts, 1), 0)
    for gi, w in enumerate(POOL_WINDOWS):
        cs = slice(gi * POOL_GROUP_DIM, (gi + 1) * POOL_GROUP_DIM)
        xg = hn[:, cs]
        tot = xg
        for j in range(1, w):
            tot = tot + xbuf[POOL_HALO - j:POOL_HALO - j + ts, cs]
        count = jnp.minimum(pos + 1, w).astype(F32)
        dg = (tot / count - xg).astype(BF16)
        y = jnp.dot(dg, pw_ref[gi], preferred_element_type=F32) * ps_ref[:, cs]
        h2_ref[:, cs] = h[:, cs] + y
    xbuf[0:POOL_HALO, :] = hn[ts - POOL_HALO:ts, :]

    h2 = h2_ref[...]
    hn2 = _rms(h2, g2_ref[...], NORM_EPS)
    logits = jnp.dot(hn2, rw_ref[...], preferred_element_type=F32,
                     precision=lax.Precision.HIGHEST)
    lane = lax.broadcasted_iota(jnp.int32, (ts, LANES), 1)
    lane_f = lane.astype(F32)
    lg = jnp.where(lane < N_EXPERTS, logits, -jnp.inf)
    m1 = jnp.max(lg, axis=-1, keepdims=True)
    i1 = jnp.min(jnp.where(lg == m1, lane_f, float(LANES)), axis=-1, keepdims=True)
    lg2 = jnp.where(lane_f == i1, -jnp.inf, lg)
    m2 = jnp.max(lg2, axis=-1, keepdims=True)
    i2 = jnp.min(jnp.where(lg2 == m2, lane_f, float(LANES)), axis=-1, keepdims=True)
    e = jnp.exp(m2 - m1)
    gate1 = 1.0 / (1.0 + e)
    gate2 = e / (1.0 + e)

    sel1 = lane_f == i1
    sel2 = lane_f == i2
    onehot = jnp.where(sel1, 1.0, jnp.where(sel2, 1.0, 0.0)).astype(BF16)
    tr = lax.broadcasted_iota(jnp.int32, (ts, ts), 0)
    tc = lax.broadcasted_iota(jnp.int32, (ts, ts), 1)
    lower = jnp.where(tc < tr, 1.0, 0.0).astype(BF16)
    before = jnp.dot(lower, onehot, preferred_element_type=F32) + carry[...]
    rank1 = jnp.sum(jnp.where(sel1, before, 0.0), axis=-1, keepdims=True)
    rank2 = jnp.sum(jnp.where(sel2, before, 0.0), axis=-1, keepdims=True)
    carry[...] = carry[...] + jnp.sum(onehot.astype(F32), axis=0, keepdims=True)
    cnt_ref[...] = jnp.broadcast_to(carry[...], cnt_ref.shape).astype(jnp.int32)

    r1 = rank1.astype(jnp.int32)
    r2 = rank2.astype(jnp.int32)
    mi_ref[...] = jnp.where(lane == 0, i1.astype(jnp.int32),
                            jnp.where(lane == 1, i2.astype(jnp.int32),
                                      jnp.where(lane == 2, r1, jnp.where(lane == 3, r2, 0))))
    mf_ref[...] = jnp.where(lane == 0, gate1, jnp.where(lane == 1, gate2, 0.0))


def _pool_router(h1, g1, pw, ps, g2, rw, bsz, seq):
    n = h1.shape[0]
    ns = seq // TS_POOL
    row = lambda b, s: (b * ns + s, 0)
    const = lambda b, s: (0, 0)
    return pl.pallas_call(
        _pool_router_kernel,
        grid=(bsz, ns),
        in_specs=[
            pl.BlockSpec((TS_POOL, D_MODEL), row),
            pl.BlockSpec((1, D_MODEL), const),
            pl.BlockSpec((len(POOL_WINDOWS), POOL_GROUP_DIM, POOL_GROUP_DIM),
                         lambda b, s: (0, 0, 0)),
            pl.BlockSpec((1, D_MODEL), const),
            pl.BlockSpec((1, D_MODEL), const),
            pl.BlockSpec((D_MODEL, LANES), const),
        ],
        out_specs=[
            pl.BlockSpec((TS_POOL, D_MODEL), row),
            pl.BlockSpec((TS_POOL, LANES), row),
            pl.BlockSpec((TS_POOL, LANES), row),
            pl.BlockSpec((8, LANES), const),
        ],
        out_shape=[
            jax.ShapeDtypeStruct((n, D_MODEL), F32),
            jax.ShapeDtypeStruct((n, LANES), jnp.int32),
            jax.ShapeDtypeStruct((n, LANES), F32),
            jax.ShapeDtypeStruct((8, LANES), jnp.int32),
        ],
        scratch_shapes=[pltpu.VMEM((POOL_HALO + TS_POOL, D_MODEL), F32),
                        pltpu.VMEM((1, LANES), F32)],
        compiler_params=_params("arbitrary", "arbitrary"),
        name="pool_router",
    )(h1, g1, pw, ps, g2, rw)


def _dispatch_kernel(pos_ref, fill_ref, h_ref, xs_hbm, zbuf, sem, fsem):
    i = pl.program_id(0)
    td = TD_DISPATCH

    def issue(r, c):
        src = h_ref.at[pl.ds(r, 1)]
        pltpu.make_async_copy(src, xs_hbm.at[pl.ds(pos_ref[0, 0, r], 1)], sem).start()
        pltpu.make_async_copy(src, xs_hbm.at[pl.ds(pos_ref[0, 0, td + r], 1)], sem).start()
        return c

    lax.fori_loop(0, td, issue, 0)

    @pl.when(i == pl.num_programs(0) - 1)
    def _():
        zbuf[...] = jnp.zeros_like(zbuf)
        zrow = zbuf.at[pl.ds(0, 1)]

        def row_copy(r):
            return pltpu.make_async_copy(zrow, xs_hbm.at[pl.ds(r, 1)], fsem)

        def tile_copy(t):
            start = pl.multiple_of(t * TM_EXP, TM_EXP)
            return pltpu.make_async_copy(zbuf, xs_hbm.at[pl.ds(start, TM_EXP)], fsem)

        def run(lo, hi, copy):
            def start(r, c):
                copy(r).start()
                return c

            def wait(r, c):
                copy(r).wait()
                return c

            lax.fori_loop(lo, hi, start, 0)
            lax.fori_loop(lo, hi, wait, 0)

        for e in range(N_EXPERTS):
            run(fill_ref[0, e], fill_ref[1, e], row_copy)
        run(fill_ref[2, 0], xs_hbm.shape[0] // TM_EXP, tile_copy)

    pltpu.make_async_copy(h_ref, xs_hbm.at[pl.ds(0, td)], sem).wait()
    pltpu.make_async_copy(h_ref, xs_hbm.at[pl.ds(0, td)], sem).wait()


def _dispatch(h2, pos3, fill, rows_padded):
    n = h2.shape[0]
    return pl.pallas_call(
        _dispatch_kernel,
        grid=(n // TD_DISPATCH,),
        in_specs=[
            pl.BlockSpec((1, 1, 2 * TD_DISPATCH), lambda i: (i, 0, 0), memory_space=pltpu.SMEM),
            pl.BlockSpec(memory_space=pltpu.SMEM),
            pl.BlockSpec((TD_DISPATCH, D_MODEL), lambda i: (i, 0)),
        ],
        out_specs=pl.BlockSpec(memory_space=pl.ANY),
        out_shape=jax.ShapeDtypeStruct((rows_padded, D_MODEL), F32),
        scratch_shapes=[pltpu.VMEM((TM_EXP, D_MODEL), F32), pltpu.SemaphoreType.DMA(()),
                        pltpu.SemaphoreType.DMA(())],
        compiler_params=_params("arbitrary"),
        name="dispatch",
    )(pos3, fill, h2)


def _experts_kernel(te_ref, tr_ref, na_ref, x_ref, g_ref, wg_ref, wu_ref, wd_ref, y_ref,
                    xn, acc):
    i = pl.program_id(0)
    j = pl.program_id(1)

    @pl.when(i < na_ref[0])
    def _():
        @pl.when(j == 0)
        def _():
            xn[...] = _rms(x_ref[...], g_ref[...], NORM_EPS).astype(BF16)
            acc[...] = jnp.zeros_like(acc)

        x = xn[...]
        gate = jnp.dot(x, wg_ref[...], preferred_element_type=F32)
        up = jnp.dot(x, wu_ref[...], preferred_element_type=F32)
        mid = (gate * _sigmoid(gate) * up).astype(BF16)
        acc[...] += jnp.dot(mid, wd_ref[...], preferred_element_type=F32)

        @pl.when(j == pl.num_programs(1) - 1)
        def _():
            y_ref[...] = acc[...]

    @pl.when((i >= na_ref[0]) & (j == 0))
    def _():
        y_ref[...] = jnp.zeros_like(y_ref)


def _experts(tile_expert, tile_row, n_active, xs, g, wg, wu, wd):
    rows = xs.shape[0]
    ntiles = rows // TM_EXP
    nf = EXPERT_DIM // TF_EXP

    def fcol(i, j, na):
        return jnp.where(i < na[0], j, nf - 1)

    grid_spec = pltpu.PrefetchScalarGridSpec(
        num_scalar_prefetch=3,
        grid=(ntiles, nf),
        in_specs=[
            pl.BlockSpec((TM_EXP, D_MODEL), lambda i, j, te, tr, na: (tr[i], 0)),
            pl.BlockSpec((1, D_MODEL), lambda i, j, te, tr, na: (0, 0)),
            pl.BlockSpec((None, D_MODEL, TF_EXP), lambda i, j, te, tr, na: (te[i], 0, fcol(i, j, na))),
            pl.BlockSpec((None, D_MODEL, TF_EXP), lambda i, j, te, tr, na: (te[i], 0, fcol(i, j, na))),
            pl.BlockSpec((None, TF_EXP, D_MODEL), lambda i, j, te, tr, na: (te[i], fcol(i, j, na), 0)),
        ],
        out_specs=pl.BlockSpec((TM_EXP, D_MODEL), lambda i, j, te, tr, na: (i, 0)),
        scratch_shapes=[pltpu.VMEM((TM_EXP, D_MODEL), BF16), pltpu.VMEM((TM_EXP, D_MODEL), F32)],
    )
    return pl.pallas_call(
        _experts_kernel,
        grid_spec=grid_spec,
        out_shape=jax.ShapeDtypeStruct((rows, D_MODEL), F32),
        compiler_params=_params("arbitrary", "arbitrary"),
        name="experts",
    )(tile_expert, tile_row, n_active, xs, g, wg, wu, wd)


def _combine_kernel(pos_ref, h_ref, mf_ref, g_ref, y_hbm, o_ref, ybuf, sem):
    tc = TC_COMBINE

    def issue(r, c):
        pltpu.make_async_copy(y_hbm.at[pl.ds(pos_ref[0, 0, r], 1)],
                              ybuf.at[0, pl.ds(r, 1)], sem).start()
        pltpu.make_async_copy(y_hbm.at[pl.ds(pos_ref[0, 0, tc + r], 1)],
                              ybuf.at[1, pl.ds(r, 1)], sem).start()
        return c

    lax.fori_loop(0, tc, issue, 0)
    blk = y_hbm.at[pl.ds(0, tc)]
    pltpu.make_async_copy(blk, ybuf.at[0], sem).wait()
    pltpu.make_async_copy(blk, ybuf.at[1], sem).wait()

    mf = mf_ref[...]
    out = h_ref[...] + mf[:, 0:1] * ybuf[0] + mf[:, 1:2] * ybuf[1]
    o_ref[...] = _rms(out, g_ref[...], NORM_EPS)


def _combine(pos3, h2, mf, g, y):
    n = h2.shape[0]
    return pl.pallas_call(
        _combine_kernel,
        grid=(n // TC_COMBINE,),
        in_specs=[
            pl.BlockSpec((1, 1, 2 * TC_COMBINE), lambda i: (i, 0, 0), memory_space=pltpu.SMEM),
            pl.BlockSpec((TC_COMBINE, D_MODEL), lambda i: (i, 0)),
            pl.BlockSpec((TC_COMBINE, LANES), lambda i: (i, 0)),
            pl.BlockSpec((1, D_MODEL), lambda i: (0, 0)),
            pl.BlockSpec(memory_space=pl.ANY),
        ],
        out_specs=pl.BlockSpec((TC_COMBINE, D_MODEL), lambda i: (i, 0)),
        out_shape=jax.ShapeDtypeStruct((n, D_MODEL), F32),
        scratch_shapes=[pltpu.VMEM((2, TC_COMBINE, D_MODEL), F32), pltpu.SemaphoreType.DMA(())],
        compiler_params=_params("arbitrary"),
        name="combine",
    )(pos3, h2, mf, g, y)


def _rope_tables(seq):
    half = ROPE_DIM // 2
    inv = ROPE_THETA ** (-jnp.arange(0, ROPE_DIM, 2, dtype=F32) / ROPE_DIM)
    ang = jnp.arange(seq, dtype=F32)[:, None] * inv[None, :]
    cos, sin = jnp.cos(ang), jnp.sin(ang)
    lane = jnp.arange(LANES) % A_QK_DIM
    idx = lane % half
    cs = jnp.where(lane < ROPE_DIM, cos[:, idx], 1.0)
    s1 = jnp.where(lane < half, -sin[:, idx], 0.0)
    s2 = jnp.where((lane >= half) & (lane < ROPE_DIM), sin[:, idx], 0.0)
    return cs, s1, s2


def _block_diag(w):
    heads, hd, _ = w.shape
    eye = jnp.eye(heads, dtype=w.dtype)
    return jnp.einsum("hij,hg->higj", w, eye).reshape(heads * hd, heads * hd)


def _routing_tables(meta_i, counts, n):
    tm = TM_EXP
    ntiles = (2 * n) // tm + N_EXPERTS
    cnt = counts[0, :N_EXPERTS]
    tiles_e = (cnt + tm - 1) // tm
    cum_tiles = jnp.cumsum(tiles_e)
    off = (cum_tiles - tiles_e) * tm
    pos1 = off[meta_i[:, 0]] + meta_i[:, 2]
    pos2 = off[meta_i[:, 1]] + meta_i[:, 3]
    n_active = cum_tiles[-1]
    t = jnp.arange(ntiles, dtype=jnp.int32)
    t_eff = jnp.minimum(t, n_active - 1)
    tile_expert = jnp.minimum(jnp.sum(t_eff[:, None] >= cum_tiles[None, :], axis=1),
                              N_EXPERTS - 1).astype(jnp.int32)
    fill = jnp.stack([off + cnt, off + tiles_e * tm,
                      jnp.broadcast_to(n_active, (N_EXPERTS,))]).astype(jnp.int32)
    return (pos1.astype(jnp.int32), pos2.astype(jnp.int32), tile_expert,
            t_eff.astype(jnp.int32), n_active.astype(jnp.int32).reshape(1), fill, ntiles * tm)


def _pos_blocks(pos1, pos2, blk):
    nb = pos1.shape[0] // blk
    return jnp.concatenate([pos1.reshape(nb, 1, blk), pos2.reshape(nb, 1, blk)], axis=-1)


def kernel(x, ln_mix_even, w_in_even, lam_q1, lam_k1, lam_q2, lam_k2, subln_g, conv_w, conv_b,
           rg_wa, rg_ba, rg_wx, rg_bx, rg_lam, w_out_even, ln_ffn_even, ffn_wg, ffn_wu, ffn_wd,
           ln_mix_odd, pool_w, pool_scale, ln_ffn_odd, router_w, moe_wg, moe_wu, moe_wd, final_g):
    bsz, seq, dm = x.shape
    n = bsz * seq
    x2 = x.reshape(n, dm)
    row = lambda v: v.reshape(1, -1)

    lam_init = 0.8 - 0.6 * math.exp(-0.3 * 0)
    cs, s1, s2 = _rope_tables(seq)
    proj = _in_proj(x2, row(ln_mix_even[0]), w_in_even[0].astype(BF16), cs, s1, s2, seq)
    vt = proj[:, 2 * A_WIDTH:3 * A_WIDTH].reshape(bsz, seq // TQ, TQ, A_WIDTH).transpose(0, 1, 3, 2)
    attn = _attention(proj, vt, row(lam_q1[0]), row(lam_k1[0]), row(lam_q2[0]), row(lam_k2[0]),
                      subln_g[0].reshape(-1, 1), bsz, seq, lam_init)
    w_gates = jnp.concatenate([_block_diag(rg_wa[0]), _block_diag(rg_wx[0])], axis=1).astype(BF16)
    b_gates = jnp.concatenate([rg_ba[0].reshape(1, -1), rg_bx[0].reshape(1, -1)], axis=1)
    rec = _rglru(proj, conv_w[0], row(conv_b[0]), w_gates, b_gates, row(rg_lam[0]), bsz, seq)
    h1 = _ffn(x2, attn, rec, w_out_even[0].astype(BF16), row(ln_ffn_even[0]),
              ffn_wg[0].astype(BF16), ffn_wu[0].astype(BF16), ffn_wd[0].astype(BF16))

    rw = jnp.pad(router_w[0], ((0, 0), (0, LANES - N_EXPERTS)))
    h2, meta_i, meta_f, counts = _pool_router(
        h1, row(ln_mix_odd[0]), pool_w[0].astype(BF16), row(pool_scale[0]),
        row(ln_ffn_odd[0]), rw, bsz, seq)
    pos1, pos2, tile_expert, tile_row, n_active, fill, rows_padded = _routing_tables(
        meta_i, counts, n)
    xs = _dispatch(h2, _pos_blocks(pos1, pos2, TD_DISPATCH), fill, rows_padded)
    y = _experts(tile_expert, tile_row, n_active, xs, row(ln_ffn_odd[0]),
                 moe_wg[0].astype(BF16), moe_wu[0].astype(BF16), moe_wd[0].astype(BF16))
    out = _combine(_pos_blocks(pos1, pos2, TC_COMBINE), h2, meta_f, row(final_g), y)
    return out.reshape(bsz, seq, dm)
```

```python
import functools
import math

import jax
import jax.numpy as jnp
from jax import lax
from jax.experimental import pallas as pl
from jax.experimental.pallas import tpu as pltpu

F32 = jnp.float32
BF16 = jnp.bfloat16

D_MODEL = 1024
A_QK_DIM = 64
A_V_DIM = 128
A_WIDTH = 512
A_HEADS = 4
ROPE_DIM = 16
ROPE_THETA = 500000.0
B_WIDTH = 512
B_HEADS = 8
B_HEAD_DIM = 64
CONV_WIDTH = 4
LRU_C = 8.0
IN_WIDTH = 2560
POOL_WINDOWS = (2, 4, 8, 16)
POOL_GROUP_DIM = 256
FFN_DIM = 2816
N_EXPERTS = 8
EXPERT_DIM = 3584
NORM_EPS = 1e-6
SUBLN_EPS = 1e-5

LANES = 128
MXU_WIDTH = 256
VMEM_LIMIT = 56 * 1024 * 1024
NEG_BIG = -1e30

TM_PROJ = 512
TQ = 256
TS_LRU = 512
TM_FFN = 512
TF_FFN = 1408
TS_POOL = 512
POOL_HALO = 16
TM_EXP = 1024
TF_EXP = 512
TD_DISPATCH = 1024
TC_COMBINE = 512
ROW_UNROLL = 8


def _params(*sem):
    return pltpu.CompilerParams(dimension_semantics=sem, vmem_limit_bytes=VMEM_LIMIT)


def _rms(x, g, eps):
    return x * lax.rsqrt(jnp.mean(x * x, axis=-1, keepdims=True) + eps) * g


def _sigmoid(x):
    return 1.0 / (1.0 + jnp.exp(-x))


def _swiglu_chunk(x, wg_ref, wu_ref, wd_ref):
    width = wg_ref.shape[-1]
    cuts = [(c, min(c + MXU_WIDTH, width)) for c in range(0, width, MXU_WIDTH)]

    def gate_up(lo, hi):
        return (jnp.dot(x, wg_ref[:, lo:hi].astype(BF16), preferred_element_type=F32),
                jnp.dot(x, wu_ref[:, lo:hi].astype(BF16), preferred_element_type=F32))

    out = None
    nxt = gate_up(*cuts[0])
    for n, (lo, hi) in enumerate(cuts):
        gate, up = nxt
        if n + 1 < len(cuts):
            nxt = gate_up(*cuts[n + 1])
        mid = (gate * _sigmoid(gate) * up).astype(BF16)
        down = jnp.dot(mid, wd_ref[lo:hi, :].astype(BF16), preferred_element_type=F32)
        out = down if out is None else out + down
    return out


def _inproj_kernel(x_ref, g_ref, w_ref, cs_ref, s1_ref, s2_ref, o_ref):
    xn = _rms(x_ref[...], g_ref[...], NORM_EPS).astype(BF16)
    cs, s1, s2 = cs_ref[...], s1_ref[...], s2_ref[...]
    half = ROPE_DIM // 2
    for c in range(IN_WIDTH // A_WIDTH):
        y = jnp.dot(xn, w_ref[:, c * A_WIDTH:(c + 1) * A_WIDTH], preferred_element_type=F32)
        if c < 2:
            scale = A_QK_DIM ** -0.5 * math.log2(math.e) if c == 0 else 1.0
            for b in range(A_WIDTH // LANES):
                yb = y[:, b * LANES:(b + 1) * LANES]
                rot = (yb * cs + pltpu.roll(yb, LANES - half, 1) * s1
                       + pltpu.roll(yb, half, 1) * s2)
                o_ref[:, c * A_WIDTH + b * LANES:c * A_WIDTH + (b + 1) * LANES] = (
                    rot * scale).astype(BF16)
        else:
            o_ref[:, c * A_WIDTH:(c + 1) * A_WIDTH] = y.astype(BF16)


def _in_proj(x2, g, w_bf, cs, s1, s2, seq):
    n = x2.shape[0]
    nblk_seq = seq // TM_PROJ
    tab = pl.BlockSpec((TM_PROJ, LANES), lambda i: (i % nblk_seq, 0))
    return pl.pallas_call(
        _inproj_kernel,
        grid=(n // TM_PROJ,),
        in_specs=[
            pl.BlockSpec((TM_PROJ, D_MODEL), lambda i: (i, 0)),
            pl.BlockSpec((1, D_MODEL), lambda i: (0, 0)),
            pl.BlockSpec((D_MODEL, IN_WIDTH), lambda i: (0, 0)),
            tab, tab, tab,
        ],
        out_specs=pl.BlockSpec((TM_PROJ, IN_WIDTH), lambda i: (i, 0)),
        out_shape=jax.ShapeDtypeStruct((n, IN_WIDTH), BF16),
        compiler_params=_params("arbitrary"),
        name="in_proj",
    )(x2, g, w_bf, cs, s1, s2)


def _attn_kernel(q_ref, k_ref, vt_ref, lq1_ref, lk1_ref, lq2_ref, lk2_ref, sg_ref, o_ref,
                 qq, m_sc, l_sc, acc, *, lam_init):
    qi = pl.program_id(1)
    lane = lax.broadcasted_iota(jnp.int32, (TQ, LANES), 1)
    for h in range(A_HEADS):
        q = q_ref[:, h * A_V_DIM:(h + 1) * A_V_DIM]
        zero = jnp.zeros_like(q)
        qq[h, 0:TQ, :] = jnp.where(lane < A_QK_DIM, q, zero)
        qq[h, TQ:2 * TQ, :] = jnp.where(lane >= A_QK_DIM, q, zero)
    m_sc[...] = jnp.full_like(m_sc, NEG_BIG)
    l_sc[...] = jnp.zeros_like(l_sc)
    acc[...] = jnp.zeros_like(acc)

    def steps(blocks):
        units = [(kv, masked, h) for kv, masked in blocks for h in range(A_HEADS)]

        def scores(kv, h):
            start = pl.multiple_of(kv * TQ, TQ)
            kb = k_ref[pl.ds(start, TQ), h * A_V_DIM:(h + 1) * A_V_DIM]
            return lax.dot_general(kb, qq[h], (((1,), (1,)), ((), ())),
                                   preferred_element_type=F32)

        st_next = scores(units[0][0], units[0][2])
        for n, (kv, masked, h) in enumerate(units):
            hs = slice(h * A_V_DIM, (h + 1) * A_V_DIM)
            st = st_next
            if n + 1 < len(units):
                st_next = scores(units[n + 1][0], units[n + 1][2])
            if masked:
                key = lax.broadcasted_iota(jnp.int32, (TQ, 2 * TQ), 0)
                qry = lax.broadcasted_iota(jnp.int32, (TQ, 2 * TQ), 1)
                st = jnp.where(key <= jnp.where(qry >= TQ, qry - TQ, qry), st, NEG_BIG)
            m_old = m_sc[h]
            m_new = jnp.maximum(m_old, jnp.max(st, axis=0, keepdims=True))
            alpha = jnp.exp2(m_old - m_new)
            p = jnp.exp2(st - m_new)
            l_sc[h] = alpha * l_sc[h] + jnp.sum(p, axis=0, keepdims=True)
            acc[h] = alpha * acc[h] + jnp.dot(vt_ref[kv, hs, :], p.astype(BF16),
                                              preferred_element_type=F32)
            m_sc[h] = m_new

    def pair(i, c):
        steps([(2 * i, False), (2 * i + 1, False)])
        return c

    lax.fori_loop(0, qi // 2, pair, 0)

    @pl.when(qi % 2 == 1)
    def _():
        steps([(qi - 1, False), (qi, True)])

    @pl.when(qi % 2 == 0)
    def _():
        steps([(qi, True)])

    lam = (jnp.exp(jnp.sum(lq1_ref[...] * lk1_ref[...], axis=-1, keepdims=True))
           - jnp.exp(jnp.sum(lq2_ref[...] * lk2_ref[...], axis=-1, keepdims=True))
           + lam_init)
    for h in range(A_HEADS):
        o = acc[h] / l_sc[h]
        o = o[:, :TQ] - lam * o[:, TQ:]
        o = o * lax.rsqrt(jnp.mean(o * o, axis=0, keepdims=True) + SUBLN_EPS) * sg_ref[...]
        o_ref[:, h * A_V_DIM:(h + 1) * A_V_DIM] = (o * (1.0 - lam_init)).T.astype(BF16)


def _attention(proj, vt, lq1, lk1, lq2, lk2, sg, bsz, seq, lam_init):
    n = proj.shape[0]
    nq = seq // TQ
    vec = pl.BlockSpec((1, A_QK_DIM), lambda b, i: (0, 0))
    return pl.pallas_call(
        functools.partial(_attn_kernel, lam_init=lam_init),
        grid=(bsz, nq),
        in_specs=[
            pl.BlockSpec((TQ, A_WIDTH), lambda b, i: (b * nq + i, 0)),
            pl.BlockSpec((seq, A_WIDTH), lambda b, i: (b, 1)),
            pl.BlockSpec((None, nq, A_WIDTH, TQ), lambda b, i: (b, 0, 0, 0)),
            vec, vec, vec, vec,
            pl.BlockSpec((A_V_DIM, 1), lambda b, i: (0, 0)),
        ],
        out_specs=pl.BlockSpec((TQ, A_WIDTH), lambda b, i: (b * nq + i, 0)),
        out_shape=jax.ShapeDtypeStruct((n, A_WIDTH), BF16),
        scratch_shapes=[pltpu.VMEM((A_HEADS, 2 * TQ, A_V_DIM), BF16),
                        pltpu.VMEM((A_HEADS, 1, 2 * TQ), F32),
                        pltpu.VMEM((A_HEADS, 1, 2 * TQ), F32),
                        pltpu.VMEM((A_HEADS, A_V_DIM, 2 * TQ), F32)],
        compiler_params=_params("arbitrary", "arbitrary"),
        name="diff_attn",
    )(proj, proj, vt, lq1, lk1, lq2, lk2, sg)


def _rglru_kernel(xb_ref, gb_ref, cw_ref, cb_ref, wg_ref, bg_ref, lam_ref, o_ref,
                  xbuf, hprev):
    si = pl.program_id(1)
    ts = TS_LRU

    @pl.when(si == 0)
    def _():
        xbuf[0:8, :] = jnp.zeros((8, B_WIDTH), F32)
        hprev[...] = jnp.zeros_like(hprev)

    x = xb_ref[...].astype(F32)
    xbuf[8:8 + ts, :] = x
    conv = x * cw_ref[CONV_WIDTH - 1:CONV_WIDTH, :] + cb_ref[...]
    for d in range(1, CONV_WIDTH):
        conv = conv + xbuf[8 - d:8 - d + ts, :] * cw_ref[CONV_WIDTH - 1 - d:CONV_WIDTH - d, :]
    xbuf[0:8, :] = x[ts - 8:ts, :]

    gates = jnp.dot(conv.astype(BF16), wg_ref[...], preferred_element_type=F32) + bg_ref[...]
    r = _sigmoid(gates[:, :B_WIDTH])
    i = _sigmoid(gates[:, B_WIDTH:])
    z = -lam_ref[...]
    softplus = jnp.maximum(z, 0.0) + jnp.log(1.0 + jnp.exp(-jnp.abs(z)))
    a = jnp.exp(-LRU_C * r * softplus)
    u = jnp.sqrt(jnp.maximum(1.0 - a * a, 0.0)) * (i * conv)

    rows = lax.broadcasted_iota(jnp.int32, (ts, B_WIDTH), 0)
    d = 1
    while d < ts:
        keep = rows >= d
        a_sh = jnp.where(keep, pltpu.roll(a, d, 0), 1.0)
        u_sh = jnp.where(keep, pltpu.roll(u, d, 0), 0.0)
        u = a * u_sh + u
        a = a * a_sh
        d *= 2
    h = u + a * hprev[...]
    hprev[...] = h[ts - 1:ts, :]

    g = gb_ref[...].astype(F32)
    gelu = 0.5 * g * (1.0 + jnp.tanh(math.sqrt(2.0 / math.pi) * (g + 0.044715 * (g * g * g))))
    o_ref[...] = (h * gelu).astype(BF16)


def _rglru(proj, conv_w, conv_b, w_gates, b_gates, lam, bsz, seq):
    n = proj.shape[0]
    ns = seq // TS_LRU
    row = lambda b, s: b * ns + s
    const = lambda b, s: (0, 0)
    return pl.pallas_call(
        _rglru_kernel,
        grid=(bsz, ns),
        in_specs=[
            pl.BlockSpec((TS_LRU, B_WIDTH), lambda b, s: (row(b, s), 3)),
            pl.BlockSpec((TS_LRU, B_WIDTH), lambda b, s: (row(b, s), 4)),
            pl.BlockSpec((CONV_WIDTH, B_WIDTH), const),
            pl.BlockSpec((1, B_WIDTH), const),
            pl.BlockSpec((B_WIDTH, 2 * B_WIDTH), const),
            pl.BlockSpec((1, 2 * B_WIDTH), const),
            pl.BlockSpec((1, B_WIDTH), const),
        ],
        out_specs=pl.BlockSpec((TS_LRU, B_WIDTH), lambda b, s: (row(b, s), 0)),
        out_shape=jax.ShapeDtypeStruct((n, B_WIDTH), BF16),
        scratch_shapes=[pltpu.VMEM((8 + TS_LRU, B_WIDTH), F32), pltpu.VMEM((1, B_WIDTH), F32)],
        compiler_params=_params("arbitrary", "arbitrary"),
        name="rglru",
    )(proj, proj, conv_w, conv_b, w_gates, b_gates, lam)


def _ffn_kernel(x_ref, at_ref, rc_ref, wo_ref, g_ref, wg_ref, wu_ref, wd_ref, o_ref, hn, acc):
    j = pl.program_id(1)

    @pl.when(j == 0)
    def _():
        h = (x_ref[...]
             + jnp.dot(at_ref[...], wo_ref[0:A_WIDTH, :], preferred_element_type=F32)
             + jnp.dot(rc_ref[...], wo_ref[A_WIDTH:, :], preferred_element_type=F32))
        hn[...] = _rms(h, g_ref[...], NORM_EPS).astype(BF16)
        acc[...] = h

    acc[...] += _swiglu_chunk(hn[...], wg_ref, wu_ref, wd_ref)

    @pl.when(j == pl.num_programs(1) - 1)
    def _():
        o_ref[...] = acc[...]


def _ffn(x2, attn, rec, wo, g, wg, wu, wd):
    n = x2.shape[0]
    const = lambda i, j: (0, 0)
    rows = lambda i, j: (i, 0)
    return pl.pallas_call(
        _ffn_kernel,
        grid=(n // TM_FFN, FFN_DIM // TF_FFN),
        in_specs=[
            pl.BlockSpec((TM_FFN, D_MODEL), rows),
            pl.BlockSpec((TM_FFN, A_WIDTH), rows),
            pl.BlockSpec((TM_FFN, B_WIDTH), rows),
            pl.BlockSpec((D_MODEL, D_MODEL), const),
            pl.BlockSpec((1, D_MODEL), const),
            pl.BlockSpec((D_MODEL, TF_FFN), lambda i, j: (0, j)),
            pl.BlockSpec((D_MODEL, TF_FFN), lambda i, j: (0, j)),
            pl.BlockSpec((TF_FFN, D_MODEL), lambda i, j: (j, 0)),
        ],
        out_specs=pl.BlockSpec((TM_FFN, D_MODEL), rows),
        out_shape=jax.ShapeDtypeStruct((n, D_MODEL), F32),
        scratch_shapes=[pltpu.VMEM((TM_FFN, D_MODEL), BF16), pltpu.VMEM((TM_FFN, D_MODEL), F32)],
        compiler_params=_params("arbitrary", "arbitrary"),
        name="outproj_ffn",
    )(x2, attn, rec, wo, g, wg, wu, wd)


def _pool_router_kernel(h_ref, g1_ref, pw_ref, ps_ref, g2_ref, rw_ref,
                        h2_ref, mi_ref, mf_ref, cnt_ref, xbuf, carry):
    b = pl.program_id(0)
    si = pl.program_id(1)
    ts = TS_POOL

    @pl.when(si == 0)
    def _():
        xbuf[0:POOL_HALO, :] = jnp.zeros((POOL_HALO, D_MODEL), F32)

    @pl.when((si == 0) & (b == 0))
    def _():
        carry[...] = jnp.zeros_like(carry)

    h = h_ref[...]
    hn = _rms(h, g1_ref[...], NORM_EPS)
    xbuf[POOL_HALO:POOL_HALO + ts, :] = hn
    pos = si * ts + lax.broadcasted_iota(jnp.int32, (ts, 1), 0)
    for gi, w in enumerate(POOL_WINDOWS):
        cs = slice(gi * POOL_GROUP_DIM, (gi + 1) * POOL_GROUP_DIM)
        xg = hn[:, cs]
        tot = xg
        for j in range(1, w):
            tot = tot + xbuf[POOL_HALO - j:POOL_HALO - j + ts, cs]
        count = jnp.minimum(pos + 1, w).astype(F32)
        dg = (tot / count - xg).astype(BF16)
        y = jnp.dot(dg, pw_ref[gi], preferred_element_type=F32) * ps_ref[:, cs]
        h2_ref[:, cs] = h[:, cs] + y
    xbuf[0:POOL_HALO, :] = hn[ts - POOL_HALO:ts, :]

    h2 = h2_ref[...]
    hn2 = _rms(h2, g2_ref[...], NORM_EPS)
    logits = jnp.dot(hn2, rw_ref[...], preferred_element_type=F32,
                     precision=lax.Precision.HIGHEST)
    lane = lax.broadcasted_iota(jnp.int32, (ts, LANES), 1)
    lane_f = lane.astype(F32)
    lg = jnp.where(lane < N_EXPERTS, logits, -jnp.inf)
    m1 = jnp.max(lg, axis=-1, keepdims=True)
    i1 = jnp.min(jnp.where(lg == m1, lane_f, float(LANES)), axis=-1, keepdims=True)
    lg2 = jnp.where(lane_f == i1, -jnp.inf, lg)
    m2 = jnp.max(lg2, axis=-1, keepdims=True)
    i2 = jnp.min(jnp.where(lg2 == m2, lane_f, float(LANES)), axis=-1, keepdims=True)
    e = jnp.exp(m2 - m1)
    gate1 = 1.0 / (1.0 + e)
    gate2 = e / (1.0 + e)

    sel1 = lane_f == i1
    sel2 = lane_f == i2
    onehot = jnp.where(sel1, 1.0, jnp.where(sel2, 1.0, 0.0)).astype(BF16)
    tr = lax.broadcasted_iota(jnp.int32, (ts, ts), 0)
    tc = lax.broadcasted_iota(jnp.int32, (ts, ts), 1)
    lower = jnp.where(tc < tr, 1.0, 0.0).astype(BF16)
    before = jnp.dot(lower, onehot, preferred_element_type=F32) + carry[...]
    rank1 = jnp.sum(jnp.where(sel1, before, 0.0), axis=-1, keepdims=True)
    rank2 = jnp.sum(jnp.where(sel2, before, 0.0), axis=-1, keepdims=True)
    carry[...] = carry[...] + jnp.sum(onehot.astype(F32), axis=0, keepdims=True)
    cnt_ref[...] = jnp.broadcast_to(carry[...], cnt_ref.shape).astype(jnp.int32)

    r1 = rank1.astype(jnp.int32)
    r2 = rank2.astype(jnp.int32)
    mi_ref[...] = jnp.where(lane == 0, i1.astype(jnp.int32),
                            jnp.where(lane == 1, i2.astype(jnp.int32),
                                      jnp.where(lane == 2, r1, jnp.where(lane == 3, r2, 0))))
    mf_ref[...] = jnp.where(lane == 0, gate1, jnp.where(lane == 1, gate2, 0.0))


def _pool_router(h1, g1, pw, ps, g2, rw, bsz, seq):
    n = h1.shape[0]
    ns = seq // TS_POOL
    row = lambda b, s: (b * ns + s, 0)
    const = lambda b, s: (0, 0)
    return pl.pallas_call(
        _pool_router_kernel,
        grid=(bsz, ns),
        in_specs=[
            pl.BlockSpec((TS_POOL, D_MODEL), row),
            pl.BlockSpec((1, D_MODEL), const),
            pl.BlockSpec((len(POOL_WINDOWS), POOL_GROUP_DIM, POOL_GROUP_DIM),
                         lambda b, s: (0, 0, 0)),
            pl.BlockSpec((1, D_MODEL), const),
            pl.BlockSpec((1, D_MODEL), const),
            pl.BlockSpec((D_MODEL, LANES), const),
        ],
        out_specs=[
            pl.BlockSpec((TS_POOL, D_MODEL), row),
            pl.BlockSpec((TS_POOL, LANES), row),
            pl.BlockSpec((TS_POOL, LANES), row),
            pl.BlockSpec((8, LANES), const),
        ],
        out_shape=[
            jax.ShapeDtypeStruct((n, D_MODEL), F32),
            jax.ShapeDtypeStruct((n, LANES), jnp.int32),
            jax.ShapeDtypeStruct((n, LANES), F32),
            jax.ShapeDtypeStruct((8, LANES), jnp.int32),
        ],
        scratch_shapes=[pltpu.VMEM((POOL_HALO + TS_POOL, D_MODEL), F32),
                        pltpu.VMEM((1, LANES), F32)],
        compiler_params=_params("arbitrary", "arbitrary"),
        name="pool_router",
    )(h1, g1, pw, ps, g2, rw)


def _dispatch_kernel(pos_ref, fill_ref, h_ref, xs_hbm, zbuf, sem, fsem):
    i = pl.program_id(0)
    td = TD_DISPATCH

    def issue(g, c):
        base = pl.multiple_of(g * ROW_UNROLL, ROW_UNROLL)
        for u in range(ROW_UNROLL):
            r = base + u
            src = h_ref.at[pl.ds(r, 1)]
            pltpu.make_async_copy(src, xs_hbm.at[pl.ds(pos_ref[0, 0, r], 1)], sem).start()
            pltpu.make_async_copy(src, xs_hbm.at[pl.ds(pos_ref[0, 0, td + r], 1)], sem).start()
        return c

    lax.fori_loop(0, td // ROW_UNROLL, issue, 0)

    @pl.when(i == pl.num_programs(0) - 1)
    def _():
        zbuf[...] = jnp.zeros_like(zbuf)
        zrow = zbuf.at[pl.ds(0, 1)]

        def row_copy(r):
            return pltpu.make_async_copy(zrow, xs_hbm.at[pl.ds(r, 1)], fsem)

        def tile_copy(t):
            start = pl.multiple_of(t * TM_EXP, TM_EXP)
            return pltpu.make_async_copy(zbuf, xs_hbm.at[pl.ds(start, TM_EXP)], fsem)

        def run(lo, hi, copy):
            def start(r, c):
                copy(r).start()
                return c

            def wait(r, c):
                copy(r).wait()
                return c

            lax.fori_loop(lo, hi, start, 0)
            lax.fori_loop(lo, hi, wait, 0)

        for e in range(N_EXPERTS):
            run(fill_ref[0, e], fill_ref[1, e], row_copy)
        run(fill_ref[2, 0], xs_hbm.shape[0] // TM_EXP, tile_copy)

    pltpu.make_async_copy(h_ref, xs_hbm.at[pl.ds(0, td)], sem).wait()
    pltpu.make_async_copy(h_ref, xs_hbm.at[pl.ds(0, td)], sem).wait()


def _dispatch(h2, pos3, fill, rows_padded):
    n = h2.shape[0]
    return pl.pallas_call(
        _dispatch_kernel,
        grid=(n // TD_DISPATCH,),
        in_specs=[
            pl.BlockSpec((1, 1, 2 * TD_DISPATCH), lambda i: (i, 0, 0), memory_space=pltpu.SMEM),
            pl.BlockSpec(memory_space=pltpu.SMEM),
            pl.BlockSpec((TD_DISPATCH, D_MODEL), lambda i: (i, 0)),
        ],
        out_specs=pl.BlockSpec(memory_space=pl.ANY),
        out_shape=jax.ShapeDtypeStruct((rows_padded, D_MODEL), F32),
        scratch_shapes=[pltpu.VMEM((TM_EXP, D_MODEL), F32), pltpu.SemaphoreType.DMA(()),
                        pltpu.SemaphoreType.DMA(())],
        compiler_params=_params("arbitrary"),
        name="dispatch",
    )(pos3, fill, h2)


def _experts_kernel(te_ref, tr_ref, na_ref, x_ref, g_ref, wg_ref, wu_ref, wd_ref, y_ref,
                    xn, acc):
    i = pl.program_id(0)
    j = pl.program_id(1)

    @pl.when(i < na_ref[0])
    def _():
        @pl.when(j == 0)
        def _():
            xn[...] = _rms(x_ref[...], g_ref[...], NORM_EPS).astype(BF16)
            acc[...] = jnp.zeros_like(acc)

        acc[...] += _swiglu_chunk(xn[...], wg_ref, wu_ref, wd_ref)

        @pl.when(j == pl.num_programs(1) - 1)
        def _():
            y_ref[...] = acc[...]

    @pl.when((i >= na_ref[0]) & (j == 0))
    def _():
        y_ref[...] = jnp.zeros_like(y_ref)


def _experts(tile_expert, tile_row, n_active, xs, g, wg, wu, wd):
    rows = xs.shape[0]
    ntiles = rows // TM_EXP
    nf = EXPERT_DIM // TF_EXP

    def fcol(i, j, na):
        return jnp.where(i < na[0], j, nf - 1)

    grid_spec = pltpu.PrefetchScalarGridSpec(
        num_scalar_prefetch=3,
        grid=(ntiles, nf),
        in_specs=[
            pl.BlockSpec((TM_EXP, D_MODEL), lambda i, j, te, tr, na: (tr[i], 0)),
            pl.BlockSpec((1, D_MODEL), lambda i, j, te, tr, na: (0, 0)),
            pl.BlockSpec((None, D_MODEL, TF_EXP), lambda i, j, te, tr, na: (te[i], 0, fcol(i, j, na))),
            pl.BlockSpec((None, D_MODEL, TF_EXP), lambda i, j, te, tr, na: (te[i], 0, fcol(i, j, na))),
            pl.BlockSpec((None, TF_EXP, D_MODEL), lambda i, j, te, tr, na: (te[i], fcol(i, j, na), 0)),
        ],
        out_specs=pl.BlockSpec((TM_EXP, D_MODEL), lambda i, j, te, tr, na: (i, 0)),
        scratch_shapes=[pltpu.VMEM((TM_EXP, D_MODEL), BF16), pltpu.VMEM((TM_EXP, D_MODEL), F32)],
    )
    return pl.pallas_call(
        _experts_kernel,
        grid_spec=grid_spec,
        out_shape=jax.ShapeDtypeStruct((rows, D_MODEL), F32),
        compiler_params=_params("arbitrary", "arbitrary"),
        name="experts",
    )(tile_expert, tile_row, n_active, xs, g, wg, wu, wd)


def _combine_kernel(pos_ref, h_ref, mf_ref, g_ref, y_hbm, o_ref, ybuf, sem):
    tc = TC_COMBINE

    def issue(g, c):
        base = pl.multiple_of(g * ROW_UNROLL, ROW_UNROLL)
        for u in range(ROW_UNROLL):
            r = base + u
            pltpu.make_async_copy(y_hbm.at[pl.ds(pos_ref[0, 0, r], 1)],
                                  ybuf.at[0, pl.ds(r, 1)], sem).start()
            pltpu.make_async_copy(y_hbm.at[pl.ds(pos_ref[0, 0, tc + r], 1)],
                                  ybuf.at[1, pl.ds(r, 1)], sem).start()
        return c

    lax.fori_loop(0, tc // ROW_UNROLL, issue, 0)
    blk = y_hbm.at[pl.ds(0, tc)]
    pltpu.make_async_copy(blk, ybuf.at[0], sem).wait()
    pltpu.make_async_copy(blk, ybuf.at[1], sem).wait()

    mf = mf_ref[...]
    out = h_ref[...] + mf[:, 0:1] * ybuf[0] + mf[:, 1:2] * ybuf[1]
    o_ref[...] = _rms(out, g_ref[...], NORM_EPS)


def _combine(pos3, h2, mf, g, y):
    n = h2.shape[0]
    return pl.pallas_call(
        _combine_kernel,
        grid=(n // TC_COMBINE,),
        in_specs=[
            pl.BlockSpec((1, 1, 2 * TC_COMBINE), lambda i: (i, 0, 0), memory_space=pltpu.SMEM),
            pl.BlockSpec((TC_COMBINE, D_MODEL), lambda i: (i, 0)),
            pl.BlockSpec((TC_COMBINE, LANES), lambda i: (i, 0)),
            pl.BlockSpec((1, D_MODEL), lambda i: (0, 0)),
            pl.BlockSpec(memory_space=pl.ANY),
        ],
        out_specs=pl.BlockSpec((TC_COMBINE, D_MODEL), lambda i: (i, 0)),
        out_shape=jax.ShapeDtypeStruct((n, D_MODEL), F32),
        scratch_shapes=[pltpu.VMEM((2, TC_COMBINE, D_MODEL), F32), pltpu.SemaphoreType.DMA(())],
        compiler_params=_params("arbitrary"),
        name="combine",
    )(pos3, h2, mf, g, y)


def _rope_tables(seq):
    half = ROPE_DIM // 2
    inv = ROPE_THETA ** (-jnp.arange(0, ROPE_DIM, 2, dtype=F32) / ROPE_DIM)
    ang = jnp.arange(seq, dtype=F32)[:, None] * inv[None, :]
    cos, sin = jnp.cos(ang), jnp.sin(ang)
    lane = jnp.arange(LANES) % A_QK_DIM
    idx = lane % half
    cs = jnp.where(lane < ROPE_DIM, cos[:, idx], 1.0)
    s1 = jnp.where(lane < half, -sin[:, idx], 0.0)
    s2 = jnp.where((lane >= half) & (lane < ROPE_DIM), sin[:, idx], 0.0)
    return cs, s1, s2


def _block_diag(w):
    heads, hd, _ = w.shape
    eye = jnp.eye(heads, dtype=w.dtype)
    return jnp.einsum("hij,hg->higj", w, eye).reshape(heads * hd, heads * hd)


def _routing_tables(meta_i, counts, n):
    tm = TM_EXP
    ntiles = (2 * n) // tm + N_EXPERTS
    cnt = counts[0, :N_EXPERTS]
    tiles_e = (cnt + tm - 1) // tm
    cum_tiles = jnp.cumsum(tiles_e)
    off = (cum_tiles - tiles_e) * tm
    pos1 = off[meta_i[:, 0]] + meta_i[:, 2]
    pos2 = off[meta_i[:, 1]] + meta_i[:, 3]
    n_active = cum_tiles[-1]
    t = jnp.arange(ntiles, dtype=jnp.int32)
    t_eff = jnp.minimum(t, n_active - 1)
    tile_expert = jnp.minimum(jnp.sum(t_eff[:, None] >= cum_tiles[None, :], axis=1),
                              N_EXPERTS - 1).astype(jnp.int32)
    fill = jnp.stack([off + cnt, off + tiles_e * tm,
                      jnp.broadcast_to(n_active, (N_EXPERTS,))]).astype(jnp.int32)
    return (pos1.astype(jnp.int32), pos2.astype(jnp.int32), tile_expert,
            t_eff.astype(jnp.int32), n_active.astype(jnp.int32).reshape(1), fill, ntiles * tm)


def _pos_blocks(pos1, pos2, blk):
    nb = pos1.shape[0] // blk
    return jnp.concatenate([pos1.reshape(nb, 1, blk), pos2.reshape(nb, 1, blk)], axis=-1)


def kernel(x, ln_mix_even, w_in_even, lam_q1, lam_k1, lam_q2, lam_k2, subln_g, conv_w, conv_b,
           rg_wa, rg_ba, rg_wx, rg_bx, rg_lam, w_out_even, ln_ffn_even, ffn_wg, ffn_wu, ffn_wd,
           ln_mix_odd, pool_w, pool_scale, ln_ffn_odd, router_w, moe_wg, moe_wu, moe_wd, final_g):
    bsz, seq, dm = x.shape
    n = bsz * seq
    x2 = x.reshape(n, dm)
    row = lambda v: v.reshape(1, -1)

    lam_init = 0.8 - 0.6 * math.exp(-0.3 * 0)
    cs, s1, s2 = _rope_tables(seq)
    proj = _in_proj(x2, row(ln_mix_even[0]), w_in_even[0].astype(BF16), cs, s1, s2, seq)
    vt = proj[:, 2 * A_WIDTH:3 * A_WIDTH].reshape(bsz, seq // TQ, TQ, A_WIDTH).transpose(0, 1, 3, 2)
    attn = _attention(proj, vt, row(lam_q1[0]), row(lam_k1[0]), row(lam_q2[0]), row(lam_k2[0]),
                      subln_g[0].reshape(-1, 1), bsz, seq, lam_init)
    w_gates = jnp.concatenate([_block_diag(rg_wa[0]), _block_diag(rg_wx[0])], axis=1).astype(BF16)
    b_gates = jnp.concatenate([rg_ba[0].reshape(1, -1), rg_bx[0].reshape(1, -1)], axis=1)
    rec = _rglru(proj, conv_w[0], row(conv_b[0]), w_gates, b_gates, row(rg_lam[0]), bsz, seq)
    h1 = _ffn(x2, attn, rec, w_out_even[0].astype(BF16), row(ln_ffn_even[0]),
              ffn_wg[0].astype(BF16), ffn_wu[0].astype(BF16), ffn_wd[0].astype(BF16))

    rw = jnp.pad(router_w[0], ((0, 0), (0, LANES - N_EXPERTS)))
    h2, meta_i, meta_f, counts = _pool_router(
        h1, row(ln_mix_odd[0]), pool_w[0].astype(BF16), row(pool_scale[0]),
        row(ln_ffn_odd[0]), rw, bsz, seq)
    pos1, pos2, tile_expert, tile_row, n_active, fill, rows_padded = _routing_tables(
        meta_i, counts, n)
    xs = _dispatch(h2, _pos_blocks(pos1, pos2, TD_DISPATCH), fill, rows_padded)
    y = _experts(tile_expert, tile_row, n_active, xs, row(ln_ffn_odd[0]),
                 moe_wg[0], moe_wu[0], moe_wd[0])
    out = _combine(_pos_blocks(pos1, pos2, TC_COMBINE), h2, meta_f, row(final_g), y)
    return out.reshape(bsz, seq, dm)
```

```python
import functools
import math

import jax
import jax.numpy as jnp
from jax import lax
from jax.experimental import pallas as pl
from jax.experimental.pallas import tpu as pltpu

F32 = jnp.float32
BF16 = jnp.bfloat16

D_MODEL = 1024
A_QK_DIM = 64
A_V_DIM = 128
A_WIDTH = 512
A_HEADS = 4
ROPE_DIM = 16
ROPE_THETA = 500000.0
B_WIDTH = 512
B_HEADS = 8
B_HEAD_DIM = 64
CONV_WIDTH = 4
LRU_C = 8.0
IN_WIDTH = 2560
POOL_WINDOWS = (2, 4, 8, 16)
POOL_GROUP_DIM = 256
FFN_DIM = 2816
N_EXPERTS = 8
EXPERT_DIM = 3584
NORM_EPS = 1e-6
SUBLN_EPS = 1e-5

LANES = 128
SUBLANES = 8
MXU_WIDTH = 256
VMEM_LIMIT = 56 * 1024 * 1024
NEG_BIG = -1e30

TM_PROJ = 512
TQ = 256
TS_LRU = 512
TM_FFN = 512
TF_FFN = 1408
TS_POOL = 512
POOL_HALO = 32
TM_EXP = 1024
TF_EXP = 512
TD_DISPATCH = 1024
TC_COMBINE = 512
ROW_UNROLL = 8


def _params(*sem):
    return pltpu.CompilerParams(dimension_semantics=sem, vmem_limit_bytes=VMEM_LIMIT)


def _rms(x, g, eps):
    return x * lax.rsqrt(jnp.mean(x * x, axis=-1, keepdims=True) + eps) * g


def _sigmoid(x):
    return 0.5 * jnp.tanh(0.5 * x) + 0.5


def _swiglu_chunk(x, wg_ref, wu_ref, wd_ref):
    width = wg_ref.shape[-1]
    cuts = [(c, min(c + MXU_WIDTH, width)) for c in range(0, width, MXU_WIDTH)]

    def gate_up(lo, hi):
        return (jnp.dot(x, wg_ref[:, lo:hi].astype(BF16), preferred_element_type=F32),
                jnp.dot(x, wu_ref[:, lo:hi].astype(BF16), preferred_element_type=F32))

    out = None
    nxt = gate_up(*cuts[0])
    for n, (lo, hi) in enumerate(cuts):
        gate, up = nxt
        if n + 1 < len(cuts):
            nxt = gate_up(*cuts[n + 1])
        mid = (gate * _sigmoid(gate) * up).astype(BF16)
        down = jnp.dot(mid, wd_ref[lo:hi, :].astype(BF16), preferred_element_type=F32)
        out = down if out is None else out + down
    return out


def _inproj_kernel(x_ref, g_ref, w_ref, cs_ref, s1_ref, s2_ref, o_ref):
    xn = _rms(x_ref[...], g_ref[...], NORM_EPS).astype(BF16)
    cs, s1, s2 = cs_ref[...], s1_ref[...], s2_ref[...]
    half = ROPE_DIM // 2
    for c in range(IN_WIDTH // A_WIDTH):
        y = jnp.dot(xn, w_ref[:, c * A_WIDTH:(c + 1) * A_WIDTH], preferred_element_type=F32)
        if c < 2:
            scale = A_QK_DIM ** -0.5 * math.log2(math.e) if c == 0 else 1.0
            for b in range(A_WIDTH // LANES):
                yb = y[:, b * LANES:(b + 1) * LANES]
                rot = (yb * cs + pltpu.roll(yb, LANES - half, 1) * s1
                       + pltpu.roll(yb, half, 1) * s2)
                o_ref[:, c * A_WIDTH + b * LANES:c * A_WIDTH + (b + 1) * LANES] = (
                    rot * scale).astype(BF16)
        else:
            o_ref[:, c * A_WIDTH:(c + 1) * A_WIDTH] = y.astype(BF16)


def _in_proj(x2, g, w_bf, cs, s1, s2, seq):
    n = x2.shape[0]
    nblk_seq = seq // TM_PROJ
    tab = pl.BlockSpec((TM_PROJ, LANES), lambda i: (i % nblk_seq, 0))
    return pl.pallas_call(
        _inproj_kernel,
        grid=(n // TM_PROJ,),
        in_specs=[
            pl.BlockSpec((TM_PROJ, D_MODEL), lambda i: (i, 0)),
            pl.BlockSpec((1, D_MODEL), lambda i: (0, 0)),
            pl.BlockSpec((D_MODEL, IN_WIDTH), lambda i: (0, 0)),
            tab, tab, tab,
        ],
        out_specs=pl.BlockSpec((TM_PROJ, IN_WIDTH), lambda i: (i, 0)),
        out_shape=jax.ShapeDtypeStruct((n, IN_WIDTH), BF16),
        compiler_params=_params("arbitrary"),
        name="in_proj",
    )(x2, g, w_bf, cs, s1, s2)


def _attn_kernel(q_ref, k_ref, vt_ref, lq1_ref, lk1_ref, lq2_ref, lk2_ref, sg_ref, o_ref,
                 qq, m_sc, l_sc, acc, *, lam_init):
    qi = pl.program_id(1)
    lane = lax.broadcasted_iota(jnp.int32, (TQ, LANES), 1)
    for h in range(A_HEADS):
        q = q_ref[:, h * A_V_DIM:(h + 1) * A_V_DIM]
        zero = jnp.zeros_like(q)
        qq[h, 0:TQ, :] = jnp.where(lane < A_QK_DIM, q, zero)
        qq[h, TQ:2 * TQ, :] = jnp.where(lane >= A_QK_DIM, q, zero)
    m_sc[...] = jnp.full_like(m_sc, NEG_BIG)
    l_sc[...] = jnp.zeros_like(l_sc)
    acc[...] = jnp.zeros_like(acc)

    def steps(blocks):
        units = [(kv, masked, h) for kv, masked in blocks for h in range(A_HEADS)]

        def scores(kv, h):
            start = pl.multiple_of(kv * TQ, TQ)
            kb = k_ref[pl.ds(start, TQ), h * A_V_DIM:(h + 1) * A_V_DIM]
            return lax.dot_general(kb, qq[h], (((1,), (1,)), ((), ())),
                                   preferred_element_type=F32)

        st_next = scores(units[0][0], units[0][2])
        for n, (kv, masked, h) in enumerate(units):
            hs = slice(h * A_V_DIM, (h + 1) * A_V_DIM)
            st = st_next
            if n + 1 < len(units):
                st_next = scores(units[n + 1][0], units[n + 1][2])
            if masked:
                key = lax.broadcasted_iota(jnp.int32, (TQ, 2 * TQ), 0)
                qry = lax.broadcasted_iota(jnp.int32, (TQ, 2 * TQ), 1)
                st = jnp.where(key <= jnp.where(qry >= TQ, qry - TQ, qry), st, NEG_BIG)
            m_old = m_sc[h]
            m_new = jnp.maximum(m_old, jnp.max(st, axis=0, keepdims=True))
            alpha = jnp.exp2(m_old - m_new)
            p = jnp.exp2(st - m_new)
            l_sc[h] = alpha * l_sc[h] + jnp.sum(p, axis=0, keepdims=True)
            acc[h] = alpha * acc[h] + jnp.dot(vt_ref[kv, hs, :], p.astype(BF16),
                                              preferred_element_type=F32)
            m_sc[h] = m_new

    def pair(i, c):
        steps([(2 * i, False), (2 * i + 1, False)])
        return c

    lax.fori_loop(0, qi // 2, pair, 0)

    @pl.when(qi % 2 == 1)
    def _():
        steps([(qi - 1, False), (qi, True)])

    @pl.when(qi % 2 == 0)
    def _():
        steps([(qi, True)])

    lam = (jnp.exp(jnp.sum(lq1_ref[...] * lk1_ref[...], axis=-1, keepdims=True))
           - jnp.exp(jnp.sum(lq2_ref[...] * lk2_ref[...], axis=-1, keepdims=True))
           + lam_init)
    for h in range(A_HEADS):
        o = acc[h] / l_sc[h]
        o = o[:, :TQ] - lam * o[:, TQ:]
        o = o * lax.rsqrt(jnp.mean(o * o, axis=0, keepdims=True) + SUBLN_EPS) * sg_ref[...]
        o_ref[:, h * A_V_DIM:(h + 1) * A_V_DIM] = (o * (1.0 - lam_init)).T.astype(BF16)


def _attention(proj, vt, lq1, lk1, lq2, lk2, sg, bsz, seq, lam_init):
    n = proj.shape[0]
    nq = seq // TQ
    vec = pl.BlockSpec((1, A_QK_DIM), lambda b, i: (0, 0))
    return pl.pallas_call(
        functools.partial(_attn_kernel, lam_init=lam_init),
        grid=(bsz, nq),
        in_specs=[
            pl.BlockSpec((TQ, A_WIDTH), lambda b, i: (b * nq + i, 0)),
            pl.BlockSpec((seq, A_WIDTH), lambda b, i: (b, 1)),
            pl.BlockSpec((None, nq, A_WIDTH, TQ), lambda b, i: (b, 0, 0, 0)),
            vec, vec, vec, vec,
            pl.BlockSpec((A_V_DIM, 1), lambda b, i: (0, 0)),
        ],
        out_specs=pl.BlockSpec((TQ, A_WIDTH), lambda b, i: (b * nq + i, 0)),
        out_shape=jax.ShapeDtypeStruct((n, A_WIDTH), BF16),
        scratch_shapes=[pltpu.VMEM((A_HEADS, 2 * TQ, A_V_DIM), BF16),
                        pltpu.VMEM((A_HEADS, 1, 2 * TQ), F32),
                        pltpu.VMEM((A_HEADS, 1, 2 * TQ), F32),
                        pltpu.VMEM((A_HEADS, A_V_DIM, 2 * TQ), F32)],
        compiler_params=_params("arbitrary", "arbitrary"),
        name="diff_attn",
    )(proj, proj, vt, lq1, lk1, lq2, lk2, sg)


def _rglru_kernel(xb_ref, gb_ref, cw_ref, cb_ref, wg_ref, bg_ref, lam_ref, o_ref,
                  xbuf, hprev, abuf, ubuf, hbuf):
    si = pl.program_id(1)
    ts = TS_LRU

    @pl.when(si == 0)
    def _():
        xbuf[0:8, :] = jnp.zeros((8, B_WIDTH), F32)
        hprev[...] = jnp.zeros_like(hprev)

    x = xb_ref[...].astype(F32)
    xbuf[8:8 + ts, :] = x
    conv = x * cw_ref[CONV_WIDTH - 1:CONV_WIDTH, :] + cb_ref[...]
    for d in range(1, CONV_WIDTH):
        conv = conv + xbuf[8 - d:8 - d + ts, :] * cw_ref[CONV_WIDTH - 1 - d:CONV_WIDTH - d, :]
    xbuf[0:8, :] = x[ts - 8:ts, :]

    gates = jnp.dot(conv.astype(BF16), wg_ref[...], preferred_element_type=F32) + bg_ref[...]
    r = _sigmoid(gates[:, :B_WIDTH])
    i = _sigmoid(gates[:, B_WIDTH:])
    z = -lam_ref[...]
    softplus = jnp.maximum(z, 0.0) + jnp.log(1.0 + jnp.exp(-jnp.abs(z)))
    a = jnp.exp(-LRU_C * r * softplus)
    om = jnp.maximum(1.0 - a * a, 0.0)
    u = jnp.where(om > 0.0, om * lax.rsqrt(om), 0.0) * (i * conv)

    nblk = B_WIDTH // LANES
    ng = ts // SUBLANES
    subs = [pl.ds(k, ng, stride=SUBLANES) for k in range(SUBLANES)]
    grow = lax.broadcasted_iota(jnp.int32, (ng, LANES), 0)
    for c in range(nblk):
        cols = slice(c * LANES, (c + 1) * LANES)
        abuf[c] = a[:, cols]
        ubuf[c] = u[:, cols]
        ga, gu = abuf[c, subs[0], :], ubuf[c, subs[0], :]
        for k in range(1, SUBLANES):
            ak = abuf[c, subs[k], :]
            gu = ak * gu + ubuf[c, subs[k], :]
            ga = ak * ga
            abuf[c, subs[k], :] = ga
            ubuf[c, subs[k], :] = gu
        d = 1
        while d < ng:
            keep = grow >= d
            ga_sh = jnp.where(keep, pltpu.roll(ga, d, 0), 1.0)
            gu_sh = jnp.where(keep, pltpu.roll(gu, d, 0), 0.0)
            gu = ga * gu_sh + gu
            ga = ga * ga_sh
            d *= 2
        h0 = hprev[:, cols]
        hg = gu + ga * h0
        h_in = jnp.where(grow >= 1, pltpu.roll(hg, 1, 0), h0)
        hprev[:, cols] = hg[ng - 1:ng, :]
        for k in range(SUBLANES):
            hbuf[c, subs[k], :] = ubuf[c, subs[k], :] + abuf[c, subs[k], :] * h_in

    g = gb_ref[...].astype(F32)
    gelu = 0.5 * g * (1.0 + jnp.tanh(math.sqrt(2.0 / math.pi) * (g + 0.044715 * (g * g * g))))
    for c in range(nblk):
        cols = slice(c * LANES, (c + 1) * LANES)
        o_ref[:, cols] = (hbuf[c] * gelu[:, cols]).astype(BF16)


def _rglru(proj, conv_w, conv_b, w_gates, b_gates, lam, bsz, seq):
    n = proj.shape[0]
    ns = seq // TS_LRU
    row = lambda b, s: b * ns + s
    const = lambda b, s: (0, 0)
    return pl.pallas_call(
        _rglru_kernel,
        grid=(bsz, ns),
        in_specs=[
            pl.BlockSpec((TS_LRU, B_WIDTH), lambda b, s: (row(b, s), 3)),
            pl.BlockSpec((TS_LRU, B_WIDTH), lambda b, s: (row(b, s), 4)),
            pl.BlockSpec((CONV_WIDTH, B_WIDTH), const),
            pl.BlockSpec((1, B_WIDTH), const),
            pl.BlockSpec((B_WIDTH, 2 * B_WIDTH), const),
            pl.BlockSpec((1, 2 * B_WIDTH), const),
            pl.BlockSpec((1, B_WIDTH), const),
        ],
        out_specs=pl.BlockSpec((TS_LRU, B_WIDTH), lambda b, s: (row(b, s), 0)),
        out_shape=jax.ShapeDtypeStruct((n, B_WIDTH), BF16),
        scratch_shapes=[pltpu.VMEM((8 + TS_LRU, B_WIDTH), F32), pltpu.VMEM((1, B_WIDTH), F32)]
        + [pltpu.VMEM((B_WIDTH // LANES, TS_LRU, LANES), F32)] * 3,
        compiler_params=_params("arbitrary", "arbitrary"),
        name="rglru",
    )(proj, proj, conv_w, conv_b, w_gates, b_gates, lam)


def _ffn_kernel(x_ref, at_ref, rc_ref, wo_ref, g_ref, wg_ref, wu_ref, wd_ref, o_ref, hn, acc):
    j = pl.program_id(1)

    @pl.when(j == 0)
    def _():
        h = (x_ref[...]
             + jnp.dot(at_ref[...], wo_ref[0:A_WIDTH, :], preferred_element_type=F32)
             + jnp.dot(rc_ref[...], wo_ref[A_WIDTH:, :], preferred_element_type=F32))
        hn[...] = _rms(h, g_ref[...], NORM_EPS).astype(BF16)
        acc[...] = h

    acc[...] += _swiglu_chunk(hn[...], wg_ref, wu_ref, wd_ref)

    @pl.when(j == pl.num_programs(1) - 1)
    def _():
        o_ref[...] = acc[...]


def _ffn(x2, attn, rec, wo, g, wg, wu, wd):
    n = x2.shape[0]
    const = lambda i, j: (0, 0)
    rows = lambda i, j: (i, 0)
    return pl.pallas_call(
        _ffn_kernel,
        grid=(n // TM_FFN, FFN_DIM // TF_FFN),
        in_specs=[
            pl.BlockSpec((TM_FFN, D_MODEL), rows),
            pl.BlockSpec((TM_FFN, A_WIDTH), rows),
            pl.BlockSpec((TM_FFN, B_WIDTH), rows),
            pl.BlockSpec((D_MODEL, D_MODEL), const),
            pl.BlockSpec((1, D_MODEL), const),
            pl.BlockSpec((D_MODEL, TF_FFN), lambda i, j: (0, j)),
            pl.BlockSpec((D_MODEL, TF_FFN), lambda i, j: (0, j)),
            pl.BlockSpec((TF_FFN, D_MODEL), lambda i, j: (j, 0)),
        ],
        out_specs=pl.BlockSpec((TM_FFN, D_MODEL), rows),
        out_shape=jax.ShapeDtypeStruct((n, D_MODEL), F32),
        scratch_shapes=[pltpu.VMEM((TM_FFN, D_MODEL), BF16), pltpu.VMEM((TM_FFN, D_MODEL), F32)],
        compiler_params=_params("arbitrary", "arbitrary"),
        name="outproj_ffn",
    )(x2, attn, rec, wo, g, wg, wu, wd)


def _pool_router_kernel(h_ref, g1_ref, pw_ref, ps_ref, g2_ref, rw_ref,
                        h2_ref, mi_ref, mf_ref, cnt_ref, xbuf, carry, sbuf):
    b = pl.program_id(0)
    si = pl.program_id(1)
    ts = TS_POOL

    @pl.when(si == 0)
    def _():
        xbuf[0:POOL_HALO, :] = jnp.zeros((POOL_HALO, D_MODEL), F32)

    @pl.when((si == 0) & (b == 0))
    def _():
        carry[...] = jnp.zeros_like(carry)

    h = h_ref[...]
    hn = _rms(h, g1_ref[...], NORM_EPS)
    xbuf[POOL_HALO:POOL_HALO + ts, :] = hn
    pos = si * ts + lax.broadcasted_iota(jnp.int32, (ts, 1), 0)
    for gi, w in enumerate(POOL_WINDOWS):
        cs = slice(gi * POOL_GROUP_DIM, (gi + 1) * POOL_GROUP_DIM)
        xg = hn[:, cs]
        src, col, d, k = xbuf, cs, 1, 0
        while 2 * d < w:
            lo = SUBLANES * (k + 1)
            dst = sbuf.at[k % 2]
            dst[lo:, :] = src[lo:, col] + src[lo - d:POOL_HALO + ts - d, col]
            src, col, d, k = dst, slice(None), 2 * d, k + 1
        tot = src[POOL_HALO:, col] + src[POOL_HALO - d:POOL_HALO + ts - d, col]
        count = jnp.minimum(pos + 1, w).astype(F32)
        dg = (tot / count - xg).astype(BF16)
        y = jnp.dot(dg, pw_ref[gi], preferred_element_type=F32) * ps_ref[:, cs]
        h2_ref[:, cs] = h[:, cs] + y
    xbuf[0:POOL_HALO, :] = hn[ts - POOL_HALO:ts, :]

    h2 = h2_ref[...]
    hn2 = _rms(h2, g2_ref[...], NORM_EPS)
    x_hi = hn2.astype(BF16)
    x_lo = (hn2 - x_hi.astype(F32)).astype(BF16)
    logits = (jnp.dot(x_hi, rw_ref[0], preferred_element_type=F32)
              + jnp.dot(x_lo, rw_ref[0], preferred_element_type=F32)
              + jnp.dot(x_hi, rw_ref[1], preferred_element_type=F32))
    lane = lax.broadcasted_iota(jnp.int32, (ts, LANES), 1)
    lane_f = lane.astype(F32)
    lg = jnp.where(lane < N_EXPERTS, logits, -jnp.inf)
    m1 = jnp.max(lg, axis=-1, keepdims=True)
    i1 = jnp.min(jnp.where(lg == m1, lane_f, float(LANES)), axis=-1, keepdims=True)
    lg2 = jnp.where(lane_f == i1, -jnp.inf, lg)
    m2 = jnp.max(lg2, axis=-1, keepdims=True)
    i2 = jnp.min(jnp.where(lg2 == m2, lane_f, float(LANES)), axis=-1, keepdims=True)
    e = jnp.exp(m2 - m1)
    gate1 = 1.0 / (1.0 + e)
    gate2 = e / (1.0 + e)

    sel1 = lane_f == i1
    sel2 = lane_f == i2
    onehot = jnp.where(sel1, 1.0, jnp.where(sel2, 1.0, 0.0)).astype(BF16)
    tr = lax.broadcasted_iota(jnp.int32, (ts, ts), 0)
    tc = lax.broadcasted_iota(jnp.int32, (ts, ts), 1)
    lower = jnp.where(tc < tr, 1.0, 0.0).astype(BF16)
    before = jnp.dot(lower, onehot, preferred_element_type=F32) + carry[...]
    rank1 = jnp.sum(jnp.where(sel1, before, 0.0), axis=-1, keepdims=True)
    rank2 = jnp.sum(jnp.where(sel2, before, 0.0), axis=-1, keepdims=True)
    carry[...] = carry[...] + jnp.sum(onehot.astype(F32), axis=0, keepdims=True)
    cnt_ref[...] = jnp.broadcast_to(carry[...], cnt_ref.shape).astype(jnp.int32)

    r1 = rank1.astype(jnp.int32)
    r2 = rank2.astype(jnp.int32)
    mi_ref[...] = jnp.where(lane == 0, i1.astype(jnp.int32),
                            jnp.where(lane == 1, i2.astype(jnp.int32),
                                      jnp.where(lane == 2, r1, jnp.where(lane == 3, r2, 0))))
    mf_ref[...] = jnp.where(lane == 0, gate1, jnp.where(lane == 1, gate2, 0.0))


def _pool_router(h1, g1, pw, ps, g2, rw, bsz, seq):
    n = h1.shape[0]
    ns = seq // TS_POOL
    row = lambda b, s: (b * ns + s, 0)
    const = lambda b, s: (0, 0)
    return pl.pallas_call(
        _pool_router_kernel,
        grid=(bsz, ns),
        in_specs=[
            pl.BlockSpec((TS_POOL, D_MODEL), row),
            pl.BlockSpec((1, D_MODEL), const),
            pl.BlockSpec((len(POOL_WINDOWS), POOL_GROUP_DIM, POOL_GROUP_DIM),
                         lambda b, s: (0, 0, 0)),
            pl.BlockSpec((1, D_MODEL), const),
            pl.BlockSpec((1, D_MODEL), const),
            pl.BlockSpec((2, D_MODEL, LANES), lambda b, s: (0, 0, 0)),
        ],
        out_specs=[
            pl.BlockSpec((TS_POOL, D_MODEL), row),
            pl.BlockSpec((TS_POOL, LANES), row),
            pl.BlockSpec((TS_POOL, LANES), row),
            pl.BlockSpec((8, LANES), const),
        ],
        out_shape=[
            jax.ShapeDtypeStruct((n, D_MODEL), F32),
            jax.ShapeDtypeStruct((n, LANES), jnp.int32),
            jax.ShapeDtypeStruct((n, LANES), F32),
            jax.ShapeDtypeStruct((8, LANES), jnp.int32),
        ],
        scratch_shapes=[pltpu.VMEM((POOL_HALO + TS_POOL, D_MODEL), F32),
                        pltpu.VMEM((1, LANES), F32),
                        pltpu.VMEM((2, POOL_HALO + TS_POOL, POOL_GROUP_DIM), F32)],
        compiler_params=_params("arbitrary", "arbitrary"),
        name="pool_router",
    )(h1, g1, pw, ps, g2, rw)


def _dispatch_kernel(pos_ref, fill_ref, h_ref, xs_hbm, zbuf, sem, fsem):
    i = pl.program_id(0)
    td = TD_DISPATCH

    def issue(g, c):
        base = pl.multiple_of(g * ROW_UNROLL, ROW_UNROLL)
        for u in range(ROW_UNROLL):
            r = base + u
            src = h_ref.at[pl.ds(r, 1)]
            pltpu.make_async_copy(src, xs_hbm.at[pl.ds(pos_ref[0, 0, r], 1)], sem).start()
            pltpu.make_async_copy(src, xs_hbm.at[pl.ds(pos_ref[0, 0, td + r], 1)], sem).start()
        return c

    lax.fori_loop(0, td // ROW_UNROLL, issue, 0)

    @pl.when(i == pl.num_programs(0) - 1)
    def _():
        zbuf[...] = jnp.zeros_like(zbuf)
        zrow = zbuf.at[pl.ds(0, 1)]

        def row_copy(r):
            return pltpu.make_async_copy(zrow, xs_hbm.at[pl.ds(r, 1)], fsem)

        def tile_copy(t):
            start = pl.multiple_of(t * TM_EXP, TM_EXP)
            return pltpu.make_async_copy(zbuf, xs_hbm.at[pl.ds(start, TM_EXP)], fsem)

        def run(lo, hi, copy):
            def start(r, c):
                copy(r).start()
                return c

            def wait(r, c):
                copy(r).wait()
                return c

            lax.fori_loop(lo, hi, start, 0)
            lax.fori_loop(lo, hi, wait, 0)

        for e in range(N_EXPERTS):
            run(fill_ref[0, e], fill_ref[1, e], row_copy)
        run(fill_ref[2, 0], xs_hbm.shape[0] // TM_EXP, tile_copy)

    pltpu.make_async_copy(h_ref, xs_hbm.at[pl.ds(0, td)], sem).wait()
    pltpu.make_async_copy(h_ref, xs_hbm.at[pl.ds(0, td)], sem).wait()


def _dispatch(h2, pos3, fill, rows_padded):
    n = h2.shape[0]
    return pl.pallas_call(
        _dispatch_kernel,
        grid=(n // TD_DISPATCH,),
        in_specs=[
            pl.BlockSpec((1, 1, 2 * TD_DISPATCH), lambda i: (i, 0, 0), memory_space=pltpu.SMEM),
            pl.BlockSpec(memory_space=pltpu.SMEM),
            pl.BlockSpec((TD_DISPATCH, D_MODEL), lambda i: (i, 0)),
        ],
        out_specs=pl.BlockSpec(memory_space=pl.ANY),
        out_shape=jax.ShapeDtypeStruct((rows_padded, D_MODEL), F32),
        scratch_shapes=[pltpu.VMEM((TM_EXP, D_MODEL), F32), pltpu.SemaphoreType.DMA(()),
                        pltpu.SemaphoreType.DMA(())],
        compiler_params=_params("arbitrary"),
        name="dispatch",
    )(pos3, fill, h2)


def _experts_kernel(te_ref, tr_ref, na_ref, x_ref, g_ref, wg_ref, wu_ref, wd_ref, y_ref,
                    xn, acc):
    i = pl.program_id(0)
    j = pl.program_id(1)

    @pl.when(i < na_ref[0])
    def _():
        @pl.when(j == 0)
        def _():
            xn[...] = _rms(x_ref[...], g_ref[...], NORM_EPS).astype(BF16)
            acc[...] = jnp.zeros_like(acc)

        acc[...] += _swiglu_chunk(xn[...], wg_ref, wu_ref, wd_ref)

        @pl.when(j == pl.num_programs(1) - 1)
        def _():
            y_ref[...] = acc[...]

    @pl.when((i >= na_ref[0]) & (j == 0))
    def _():
        y_ref[...] = jnp.zeros_like(y_ref)


def _experts(tile_expert, tile_row, n_active, xs, g, wg, wu, wd):
    rows = xs.shape[0]
    ntiles = rows // TM_EXP
    nf = EXPERT_DIM // TF_EXP

    def fcol(i, j, na):
        return jnp.where(i < na[0], j, nf - 1)

    grid_spec = pltpu.PrefetchScalarGridSpec(
        num_scalar_prefetch=3,
        grid=(ntiles, nf),
        in_specs=[
            pl.BlockSpec((TM_EXP, D_MODEL), lambda i, j, te, tr, na: (tr[i], 0)),
            pl.BlockSpec((1, D_MODEL), lambda i, j, te, tr, na: (0, 0)),
            pl.BlockSpec((None, D_MODEL, TF_EXP), lambda i, j, te, tr, na: (te[i], 0, fcol(i, j, na))),
            pl.BlockSpec((None, D_MODEL, TF_EXP), lambda i, j, te, tr, na: (te[i], 0, fcol(i, j, na))),
            pl.BlockSpec((None, TF_EXP, D_MODEL), lambda i, j, te, tr, na: (te[i], fcol(i, j, na), 0)),
        ],
        out_specs=pl.BlockSpec((TM_EXP, D_MODEL), lambda i, j, te, tr, na: (i, 0)),
        scratch_shapes=[pltpu.VMEM((TM_EXP, D_MODEL), BF16), pltpu.VMEM((TM_EXP, D_MODEL), F32)],
    )
    return pl.pallas_call(
        _experts_kernel,
        grid_spec=grid_spec,
        out_shape=jax.ShapeDtypeStruct((rows, D_MODEL), F32),
        compiler_params=_params("arbitrary", "arbitrary"),
        name="experts",
    )(tile_expert, tile_row, n_active, xs, g, wg, wu, wd)


def _combine_kernel(pos_ref, h_ref, mf_ref, g_ref, y_hbm, o_ref, ybuf, sem):
    tc = TC_COMBINE

    def issue(g, c):
        base = pl.multiple_of(g * ROW_UNROLL, ROW_UNROLL)
        for u in range(ROW_UNROLL):
            r = base + u
            pltpu.make_async_copy(y_hbm.at[pl.ds(pos_ref[0, 0, r], 1)],
                                  ybuf.at[0, pl.ds(r, 1)], sem).start()
            pltpu.make_async_copy(y_hbm.at[pl.ds(pos_ref[0, 0, tc + r], 1)],
                                  ybuf.at[1, pl.ds(r, 1)], sem).start()
        return c

    lax.fori_loop(0, tc // ROW_UNROLL, issue, 0)
    blk = y_hbm.at[pl.ds(0, tc)]
    pltpu.make_async_copy(blk, ybuf.at[0], sem).wait()
    pltpu.make_async_copy(blk, ybuf.at[1], sem).wait()

    mf = mf_ref[...]
    out = h_ref[...] + mf[:, 0:1] * ybuf[0] + mf[:, 1:2] * ybuf[1]
    o_ref[...] = _rms(out, g_ref[...], NORM_EPS)


def _combine(pos3, h2, mf, g, y):
    n = h2.shape[0]
    return pl.pallas_call(
        _combine_kernel,
        grid=(n // TC_COMBINE,),
        in_specs=[
            pl.BlockSpec((1, 1, 2 * TC_COMBINE), lambda i: (i, 0, 0), memory_space=pltpu.SMEM),
            pl.BlockSpec((TC_COMBINE, D_MODEL), lambda i: (i, 0)),
            pl.BlockSpec((TC_COMBINE, LANES), lambda i: (i, 0)),
            pl.BlockSpec((1, D_MODEL), lambda i: (0, 0)),
            pl.BlockSpec(memory_space=pl.ANY),
        ],
        out_specs=pl.BlockSpec((TC_COMBINE, D_MODEL), lambda i: (i, 0)),
        out_shape=jax.ShapeDtypeStruct((n, D_MODEL), F32),
        scratch_shapes=[pltpu.VMEM((2, TC_COMBINE, D_MODEL), F32), pltpu.SemaphoreType.DMA(())],
        compiler_params=_params("arbitrary"),
        name="combine",
    )(pos3, h2, mf, g, y)


def _rope_tables(seq):
    half = ROPE_DIM // 2
    inv = ROPE_THETA ** (-jnp.arange(0, ROPE_DIM, 2, dtype=F32) / ROPE_DIM)
    ang = jnp.arange(seq, dtype=F32)[:, None] * inv[None, :]
    cos, sin = jnp.cos(ang), jnp.sin(ang)
    lane = jnp.arange(LANES) % A_QK_DIM
    idx = lane % half
    cs = jnp.where(lane < ROPE_DIM, cos[:, idx], 1.0)
    s1 = jnp.where(lane < half, -sin[:, idx], 0.0)
    s2 = jnp.where((lane >= half) & (lane < ROPE_DIM), sin[:, idx], 0.0)
    return cs, s1, s2


def _block_diag(w):
    heads, hd, _ = w.shape
    eye = jnp.eye(heads, dtype=w.dtype)
    return jnp.einsum("hij,hg->higj", w, eye).reshape(heads * hd, heads * hd)


def _routing_tables(meta_i, counts, n):
    tm = TM_EXP
    ntiles = (2 * n) // tm + N_EXPERTS
    cnt = counts[0, :N_EXPERTS]
    tiles_e = (cnt + tm - 1) // tm
    cum_tiles = jnp.cumsum(tiles_e)
    off = (cum_tiles - tiles_e) * tm
    pos1 = off[meta_i[:, 0]] + meta_i[:, 2]
    pos2 = off[meta_i[:, 1]] + meta_i[:, 3]
    n_active = cum_tiles[-1]
    t = jnp.arange(ntiles, dtype=jnp.int32)
    t_eff = jnp.minimum(t, n_active - 1)
    tile_expert = jnp.minimum(jnp.sum(t_eff[:, None] >= cum_tiles[None, :], axis=1),
                              N_EXPERTS - 1).astype(jnp.int32)
    fill = jnp.stack([off + cnt, off + tiles_e * tm,
                      jnp.broadcast_to(n_active, (N_EXPERTS,))]).astype(jnp.int32)
    return (pos1.astype(jnp.int32), pos2.astype(jnp.int32), tile_expert,
            t_eff.astype(jnp.int32), n_active.astype(jnp.int32).reshape(1), fill, ntiles * tm)


def _pos_blocks(pos1, pos2, blk):
    nb = pos1.shape[0] // blk
    return jnp.concatenate([pos1.reshape(nb, 1, blk), pos2.reshape(nb, 1, blk)], axis=-1)


def kernel(x, ln_mix_even, w_in_even, lam_q1, lam_k1, lam_q2, lam_k2, subln_g, conv_w, conv_b,
           rg_wa, rg_ba, rg_wx, rg_bx, rg_lam, w_out_even, ln_ffn_even, ffn_wg, ffn_wu, ffn_wd,
           ln_mix_odd, pool_w, pool_scale, ln_ffn_odd, router_w, moe_wg, moe_wu, moe_wd, final_g):
    bsz, seq, dm = x.shape
    n = bsz * seq
    x2 = x.reshape(n, dm)
    row = lambda v: v.reshape(1, -1)

    lam_init = 0.8 - 0.6 * math.exp(-0.3 * 0)
    cs, s1, s2 = _rope_tables(seq)
    proj = _in_proj(x2, row(ln_mix_even[0]), w_in_even[0].astype(BF16), cs, s1, s2, seq)
    vt = proj[:, 2 * A_WIDTH:3 * A_WIDTH].reshape(bsz, seq // TQ, TQ, A_WIDTH).transpose(0, 1, 3, 2)
    attn = _attention(proj, vt, row(lam_q1[0]), row(lam_k1[0]), row(lam_q2[0]), row(lam_k2[0]),
                      subln_g[0].reshape(-1, 1), bsz, seq, lam_init)
    w_gates = jnp.concatenate([_block_diag(rg_wa[0]), _block_diag(rg_wx[0])], axis=1).astype(BF16)
    b_gates = jnp.concatenate([rg_ba[0].reshape(1, -1), rg_bx[0].reshape(1, -1)], axis=1)
    rec = _rglru(proj, conv_w[0], row(conv_b[0]), w_gates, b_gates, row(rg_lam[0]), bsz, seq)
    h1 = _ffn(x2, attn, rec, w_out_even[0].astype(BF16), row(ln_ffn_even[0]),
              ffn_wg[0].astype(BF16), ffn_wu[0].astype(BF16), ffn_wd[0].astype(BF16))

    rw = jnp.pad(router_w[0], ((0, 0), (0, LANES - N_EXPERTS)))
    rw_hi = rw.astype(BF16)
    rw = jnp.stack([rw_hi, (rw - rw_hi.astype(F32)).astype(BF16)])
    h2, meta_i, meta_f, counts = _pool_router(
        h1, row(ln_mix_odd[0]), pool_w[0].astype(BF16), row(pool_scale[0]),
        row(ln_ffn_odd[0]), rw, bsz, seq)
    pos1, pos2, tile_expert, tile_row, n_active, fill, rows_padded = _routing_tables(
        meta_i, counts, n)
    xs = _dispatch(h2, _pos_blocks(pos1, pos2, TD_DISPATCH), fill, rows_padded)
    y = _experts(tile_expert, tile_row, n_active, xs, row(ln_ffn_odd[0]),
                 moe_wg[0], moe_wu[0], moe_wd[0])
    out = _combine(_pos_blocks(pos1, pos2, TC_COMBINE), h2, meta_f, row(final_g), y)
    return out.reshape(bsz, seq, dm)
```

```python
import functools
import math

import jax
import jax.numpy as jnp
from jax import lax
from jax.experimental import pallas as pl
from jax.experimental.pallas import tpu as pltpu

F32 = jnp.float32
BF16 = jnp.bfloat16

D_MODEL = 1024
A_QK_DIM = 64
A_V_DIM = 128
A_WIDTH = 512
A_HEADS = 4
ROPE_DIM = 16
ROPE_THETA = 500000.0
B_WIDTH = 512
B_HEADS = 8
B_HEAD_DIM = 64
CONV_WIDTH = 4
LRU_C = 8.0
IN_WIDTH = 2560
POOL_WINDOWS = (2, 4, 8, 16)
POOL_GROUP_DIM = 256
FFN_DIM = 2816
N_EXPERTS = 8
EXPERT_DIM = 3584
NORM_EPS = 1e-6
SUBLN_EPS = 1e-5

LANES = 128
SUBLANES = 8
MXU_WIDTH = 256
VMEM_LIMIT = 56 * 1024 * 1024
NEG_BIG = -1e30

TM_PROJ = 512
TQ = 256
TS_LRU = 512
TM_FFN = 512
TF_FFN = 1408
TS_POOL = 512
POOL_HALO = 32
TM_EXP = 1024
TF_EXP = 512
TD_DISPATCH = 1024
TC_COMBINE = 512
ROW_UNROLL = 8


def _params(*sem):
    return pltpu.CompilerParams(dimension_semantics=sem, vmem_limit_bytes=VMEM_LIMIT)


def _rms(x, g, eps):
    return x * lax.rsqrt(jnp.mean(x * x, axis=-1, keepdims=True) + eps) * g


def _sigmoid(x):
    return 0.5 * jnp.tanh(0.5 * x) + 0.5


def _swiglu_chunk(x, wg_ref, wu_ref, wd_ref):
    width = wg_ref.shape[-1]
    cuts = [(c, min(c + MXU_WIDTH, width)) for c in range(0, width, MXU_WIDTH)]

    def gate_up(lo, hi):
        return (jnp.dot(x, wg_ref[:, lo:hi].astype(BF16), preferred_element_type=F32),
                jnp.dot(x, wu_ref[:, lo:hi].astype(BF16), preferred_element_type=F32))

    out = None
    nxt = gate_up(*cuts[0])
    for n, (lo, hi) in enumerate(cuts):
        gate, up = nxt
        if n + 1 < len(cuts):
            nxt = gate_up(*cuts[n + 1])
        mid = (gate * _sigmoid(gate) * up).astype(BF16)
        down = jnp.dot(mid, wd_ref[lo:hi, :].astype(BF16), preferred_element_type=F32)
        out = down if out is None else out + down
    return out


def _inproj_kernel(x_ref, g_ref, w_ref, cs_ref, s1_ref, s2_ref, o_ref, vt_ref):
    xn = _rms(x_ref[...], g_ref[...], NORM_EPS).astype(BF16)
    cs, s1, s2 = cs_ref[...], s1_ref[...], s2_ref[...]
    half = ROPE_DIM // 2
    for c in range(IN_WIDTH // A_WIDTH):
        y = jnp.dot(xn, w_ref[:, c * A_WIDTH:(c + 1) * A_WIDTH], preferred_element_type=F32)
        if c < 2:
            scale = A_QK_DIM ** -0.5 * math.log2(math.e) if c == 0 else 1.0
            for b in range(A_WIDTH // LANES):
                yb = y[:, b * LANES:(b + 1) * LANES]
                rot = (yb * cs + pltpu.roll(yb, LANES - half, 1) * s1
                       + pltpu.roll(yb, half, 1) * s2)
                o_ref[:, c * A_WIDTH + b * LANES:c * A_WIDTH + (b + 1) * LANES] = (
                    rot * scale).astype(BF16)
        elif c == 2:
            for t in range(TM_PROJ // TQ):
                vt_ref[t] = y[t * TQ:(t + 1) * TQ, :].T.astype(BF16)
        else:
            o_ref[:, (c - 1) * A_WIDTH:c * A_WIDTH] = y.astype(BF16)


def _in_proj(x2, g, w_bf, cs, s1, s2, seq):
    n = x2.shape[0]
    nblk_seq = seq // TM_PROJ
    tab = pl.BlockSpec((TM_PROJ, LANES), lambda i: (i % nblk_seq, 0))
    return pl.pallas_call(
        _inproj_kernel,
        grid=(n // TM_PROJ,),
        in_specs=[
            pl.BlockSpec((TM_PROJ, D_MODEL), lambda i: (i, 0)),
            pl.BlockSpec((1, D_MODEL), lambda i: (0, 0)),
            pl.BlockSpec((D_MODEL, IN_WIDTH), lambda i: (0, 0)),
            tab, tab, tab,
        ],
        out_specs=[
            pl.BlockSpec((TM_PROJ, IN_WIDTH - A_WIDTH), lambda i: (i, 0)),
            pl.BlockSpec((None, TM_PROJ // TQ, A_WIDTH, TQ),
                         lambda i: (i // nblk_seq, i % nblk_seq, 0, 0)),
        ],
        out_shape=[
            jax.ShapeDtypeStruct((n, IN_WIDTH - A_WIDTH), BF16),
            jax.ShapeDtypeStruct((n // seq, seq // TQ, A_WIDTH, TQ), BF16),
        ],
        compiler_params=_params("arbitrary"),
        name="in_proj",
    )(x2, g, w_bf, cs, s1, s2)


def _attn_kernel(q_ref, k_ref, vt_ref, lq1_ref, lk1_ref, lq2_ref, lk2_ref, sg_ref, o_ref,
                 qq, m_sc, l_sc, acc, *, lam_init):
    qi = pl.program_id(1)
    lane = lax.broadcasted_iota(jnp.int32, (TQ, LANES), 1)
    for h in range(A_HEADS):
        q = q_ref[:, h * A_V_DIM:(h + 1) * A_V_DIM]
        zero = jnp.zeros_like(q)
        qq[h, 0:TQ, :] = jnp.where(lane < A_QK_DIM, q, zero)
        qq[h, TQ:2 * TQ, :] = jnp.where(lane >= A_QK_DIM, q, zero)
    m_sc[...] = jnp.full_like(m_sc, NEG_BIG)
    l_sc[...] = jnp.zeros_like(l_sc)
    acc[...] = jnp.zeros_like(acc)

    def steps(blocks):
        units = [(kv, masked, h) for kv, masked in blocks for h in range(A_HEADS)]

        def scores(kv, h):
            start = pl.multiple_of(kv * TQ, TQ)
            kb = k_ref[pl.ds(start, TQ), h * A_V_DIM:(h + 1) * A_V_DIM]
            return lax.dot_general(kb, qq[h], (((1,), (1,)), ((), ())),
                                   preferred_element_type=F32)

        st_next = scores(units[0][0], units[0][2])
        for n, (kv, masked, h) in enumerate(units):
            hs = slice(h * A_V_DIM, (h + 1) * A_V_DIM)
            st = st_next
            if n + 1 < len(units):
                st_next = scores(units[n + 1][0], units[n + 1][2])
            if masked:
                key = lax.broadcasted_iota(jnp.int32, (TQ, 2 * TQ), 0)
                qry = lax.broadcasted_iota(jnp.int32, (TQ, 2 * TQ), 1)
                st = jnp.where(key <= jnp.where(qry >= TQ, qry - TQ, qry), st, NEG_BIG)
            m_old = m_sc[h]
            m_new = jnp.maximum(m_old, jnp.max(st, axis=0, keepdims=True))
            alpha = jnp.exp2(m_old - m_new)
            p = jnp.exp2(st - m_new)
            l_sc[h] = alpha * l_sc[h] + jnp.sum(p, axis=0, keepdims=True)
            acc[h] = alpha * acc[h] + jnp.dot(vt_ref[kv, hs, :], p.astype(BF16),
                                              preferred_element_type=F32)
            m_sc[h] = m_new

    def pair(i, c):
        steps([(2 * i, False), (2 * i + 1, False)])
        return c

    lax.fori_loop(0, qi // 2, pair, 0)

    @pl.when(qi % 2 == 1)
    def _():
        steps([(qi - 1, False), (qi, True)])

    @pl.when(qi % 2 == 0)
    def _():
        steps([(qi, True)])

    lam = (jnp.exp(jnp.sum(lq1_ref[...] * lk1_ref[...], axis=-1, keepdims=True))
           - jnp.exp(jnp.sum(lq2_ref[...] * lk2_ref[...], axis=-1, keepdims=True))
           + lam_init)
    for h in range(A_HEADS):
        o = acc[h] / l_sc[h]
        o = o[:, :TQ] - lam * o[:, TQ:]
        o = o * lax.rsqrt(jnp.mean(o * o, axis=0, keepdims=True) + SUBLN_EPS) * sg_ref[...]
        o_ref[:, h * A_V_DIM:(h + 1) * A_V_DIM] = (o * (1.0 - lam_init)).T.astype(BF16)


def _attention(proj, vt, lq1, lk1, lq2, lk2, sg, bsz, seq, lam_init):
    n = proj.shape[0]
    nq = seq // TQ
    vec = pl.BlockSpec((1, A_QK_DIM), lambda b, i: (0, 0))
    return pl.pallas_call(
        functools.partial(_attn_kernel, lam_init=lam_init),
        grid=(bsz, nq),
        in_specs=[
            pl.BlockSpec((TQ, A_WIDTH), lambda b, i: (b * nq + i, 0)),
            pl.BlockSpec((seq, A_WIDTH), lambda b, i: (b, 1)),
            pl.BlockSpec((None, nq, A_WIDTH, TQ), lambda b, i: (b, 0, 0, 0)),
            vec, vec, vec, vec,
            pl.BlockSpec((A_V_DIM, 1), lambda b, i: (0, 0)),
        ],
        out_specs=pl.BlockSpec((TQ, A_WIDTH), lambda b, i: (b * nq + i, 0)),
        out_shape=jax.ShapeDtypeStruct((n, A_WIDTH), BF16),
        scratch_shapes=[pltpu.VMEM((A_HEADS, 2 * TQ, A_V_DIM), BF16),
                        pltpu.VMEM((A_HEADS, 1, 2 * TQ), F32),
                        pltpu.VMEM((A_HEADS, 1, 2 * TQ), F32),
                        pltpu.VMEM((A_HEADS, A_V_DIM, 2 * TQ), F32)],
        compiler_params=_params("arbitrary", "arbitrary"),
        name="diff_attn",
    )(proj, proj, vt, lq1, lk1, lq2, lk2, sg)


def _rglru_kernel(xb_ref, gb_ref, cw_ref, cb_ref, wg_ref, bg_ref, lam_ref, o_ref,
                  xbuf, hprev, abuf, ubuf, hbuf):
    si = pl.program_id(1)
    ts = TS_LRU

    @pl.when(si == 0)
    def _():
        xbuf[0:8, :] = jnp.zeros((8, B_WIDTH), F32)
        hprev[...] = jnp.zeros_like(hprev)

    x = xb_ref[...].astype(F32)
    xbuf[8:8 + ts, :] = x
    conv = x * cw_ref[CONV_WIDTH - 1:CONV_WIDTH, :] + cb_ref[...]
    for d in range(1, CONV_WIDTH):
        conv = conv + xbuf[8 - d:8 - d + ts, :] * cw_ref[CONV_WIDTH - 1 - d:CONV_WIDTH - d, :]
    xbuf[0:8, :] = x[ts - 8:ts, :]

    gates = jnp.dot(conv.astype(BF16), wg_ref[...], preferred_element_type=F32) + bg_ref[...]
    r = _sigmoid(gates[:, :B_WIDTH])
    i = _sigmoid(gates[:, B_WIDTH:])
    z = -lam_ref[...]
    softplus = jnp.maximum(z, 0.0) + jnp.log(1.0 + jnp.exp(-jnp.abs(z)))
    a = jnp.exp(-LRU_C * r * softplus)
    om = jnp.maximum(1.0 - a * a, 0.0)
    u = jnp.where(om > 0.0, om * lax.rsqrt(om), 0.0) * (i * conv)

    nblk = B_WIDTH // LANES
    ng = ts // SUBLANES
    subs = [pl.ds(k, ng, stride=SUBLANES) for k in range(SUBLANES)]
    grow = lax.broadcasted_iota(jnp.int32, (ng, LANES), 0)
    for c in range(nblk):
        cols = slice(c * LANES, (c + 1) * LANES)
        abuf[c] = a[:, cols]
        ubuf[c] = u[:, cols]
        ga, gu = abuf[c, subs[0], :], ubuf[c, subs[0], :]
        for k in range(1, SUBLANES):
            ak = abuf[c, subs[k], :]
            gu = ak * gu + ubuf[c, subs[k], :]
            ga = ak * ga
            abuf[c, subs[k], :] = ga
            ubuf[c, subs[k], :] = gu
        d = 1
        while d < ng:
            keep = grow >= d
            ga_sh = jnp.where(keep, pltpu.roll(ga, d, 0), 1.0)
            gu_sh = jnp.where(keep, pltpu.roll(gu, d, 0), 0.0)
            gu = ga * gu_sh + gu
            ga = ga * ga_sh
            d *= 2
        h0 = hprev[:, cols]
        hg = gu + ga * h0
        h_in = jnp.where(grow >= 1, pltpu.roll(hg, 1, 0), h0)
        hprev[:, cols] = hg[ng - 1:ng, :]
        for k in range(SUBLANES):
            hbuf[c, subs[k], :] = ubuf[c, subs[k], :] + abuf[c, subs[k], :] * h_in

    g = gb_ref[...].astype(F32)
    gelu = 0.5 * g * (1.0 + jnp.tanh(math.sqrt(2.0 / math.pi) * (g + 0.044715 * (g * g * g))))
    for c in range(nblk):
        cols = slice(c * LANES, (c + 1) * LANES)
        o_ref[:, cols] = (hbuf[c] * gelu[:, cols]).astype(BF16)


def _rglru(proj, conv_w, conv_b, w_gates, b_gates, lam, bsz, seq):
    n = proj.shape[0]
    ns = seq // TS_LRU
    row = lambda b, s: b * ns + s
    const = lambda b, s: (0, 0)
    return pl.pallas_call(
        _rglru_kernel,
        grid=(bsz, ns),
        in_specs=[
            pl.BlockSpec((TS_LRU, B_WIDTH), lambda b, s: (row(b, s), 2)),
            pl.BlockSpec((TS_LRU, B_WIDTH), lambda b, s: (row(b, s), 3)),
            pl.BlockSpec((CONV_WIDTH, B_WIDTH), const),
            pl.BlockSpec((1, B_WIDTH), const),
            pl.BlockSpec((B_WIDTH, 2 * B_WIDTH), const),
            pl.BlockSpec((1, 2 * B_WIDTH), const),
            pl.BlockSpec((1, B_WIDTH), const),
        ],
        out_specs=pl.BlockSpec((TS_LRU, B_WIDTH), lambda b, s: (row(b, s), 0)),
        out_shape=jax.ShapeDtypeStruct((n, B_WIDTH), BF16),
        scratch_shapes=[pltpu.VMEM((8 + TS_LRU, B_WIDTH), F32), pltpu.VMEM((1, B_WIDTH), F32)]
        + [pltpu.VMEM((B_WIDTH // LANES, TS_LRU, LANES), F32)] * 3,
        compiler_params=_params("arbitrary", "arbitrary"),
        name="rglru",
    )(proj, proj, conv_w, conv_b, w_gates, b_gates, lam)


def _ffn_kernel(x_ref, at_ref, rc_ref, wo_ref, g_ref, wg_ref, wu_ref, wd_ref, o_ref, hn, acc):
    j = pl.program_id(1)

    @pl.when(j == 0)
    def _():
        h = (x_ref[...]
             + jnp.dot(at_ref[...], wo_ref[0:A_WIDTH, :], preferred_element_type=F32)
             + jnp.dot(rc_ref[...], wo_ref[A_WIDTH:, :], preferred_element_type=F32))
        hn[...] = _rms(h, g_ref[...], NORM_EPS).astype(BF16)
        acc[...] = h

    acc[...] += _swiglu_chunk(hn[...], wg_ref, wu_ref, wd_ref)

    @pl.when(j == pl.num_programs(1) - 1)
    def _():
        o_ref[...] = acc[...]


def _ffn(x2, attn, rec, wo, g, wg, wu, wd):
    n = x2.shape[0]
    const = lambda i, j: (0, 0)
    rows = lambda i, j: (i, 0)
    return pl.pallas_call(
        _ffn_kernel,
        grid=(n // TM_FFN, FFN_DIM // TF_FFN),
        in_specs=[
            pl.BlockSpec((TM_FFN, D_MODEL), rows),
            pl.BlockSpec((TM_FFN, A_WIDTH), rows),
            pl.BlockSpec((TM_FFN, B_WIDTH), rows),
            pl.BlockSpec((D_MODEL, D_MODEL), const),
            pl.BlockSpec((1, D_MODEL), const),
            pl.BlockSpec((D_MODEL, TF_FFN), lambda i, j: (0, j)),
            pl.BlockSpec((D_MODEL, TF_FFN), lambda i, j: (0, j)),
            pl.BlockSpec((TF_FFN, D_MODEL), lambda i, j: (j, 0)),
        ],
        out_specs=pl.BlockSpec((TM_FFN, D_MODEL), rows),
        out_shape=jax.ShapeDtypeStruct((n, D_MODEL), F32),
        scratch_shapes=[pltpu.VMEM((TM_FFN, D_MODEL), BF16), pltpu.VMEM((TM_FFN, D_MODEL), F32)],
        compiler_params=_params("arbitrary", "arbitrary"),
        name="outproj_ffn",
    )(x2, attn, rec, wo, g, wg, wu, wd)


def _pool_router_kernel(h_ref, g1_ref, pw_ref, ps_ref, g2_ref, rw_ref,
                        h2_ref, mi_ref, mf_ref, cnt_ref, xbuf, carry, sbuf):
    b = pl.program_id(0)
    si = pl.program_id(1)
    ts = TS_POOL

    @pl.when(si == 0)
    def _():
        xbuf[0:POOL_HALO, :] = jnp.zeros((POOL_HALO, D_MODEL), F32)

    @pl.when((si == 0) & (b == 0))
    def _():
        carry[...] = jnp.zeros_like(carry)

    h = h_ref[...]
    hn = _rms(h, g1_ref[...], NORM_EPS)
    xbuf[POOL_HALO:POOL_HALO + ts, :] = hn
    pos = si * ts + lax.broadcasted_iota(jnp.int32, (ts, 1), 0)
    for gi, w in enumerate(POOL_WINDOWS):
        cs = slice(gi * POOL_GROUP_DIM, (gi + 1) * POOL_GROUP_DIM)
        xg = hn[:, cs]
        src, col, d, k = xbuf, cs, 1, 0
        while 2 * d < w:
            lo = SUBLANES * (k + 1)
            dst = sbuf.at[k % 2]
            dst[lo:, :] = src[lo:, col] + src[lo - d:POOL_HALO + ts - d, col]
            src, col, d, k = dst, slice(None), 2 * d, k + 1
        tot = src[POOL_HALO:, col] + src[POOL_HALO - d:POOL_HALO + ts - d, col]
        count = jnp.minimum(pos + 1, w).astype(F32)
        dg = (tot / count - xg).astype(BF16)
        y = jnp.dot(dg, pw_ref[gi], preferred_element_type=F32) * ps_ref[:, cs]
        h2_ref[:, cs] = h[:, cs] + y
    xbuf[0:POOL_HALO, :] = hn[ts - POOL_HALO:ts, :]

    h2 = h2_ref[...]
    hn2 = _rms(h2, g2_ref[...], NORM_EPS)
    x_hi = hn2.astype(BF16)
    x_lo = (hn2 - x_hi.astype(F32)).astype(BF16)
    logits = (jnp.dot(x_hi, rw_ref[0], preferred_element_type=F32)
              + jnp.dot(x_lo, rw_ref[0], preferred_element_type=F32)
              + jnp.dot(x_hi, rw_ref[1], preferred_element_type=F32))
    lane = lax.broadcasted_iota(jnp.int32, (ts, LANES), 1)
    lane_f = lane.astype(F32)
    lg = jnp.where(lane < N_EXPERTS, logits, -jnp.inf)
    m1 = jnp.max(lg, axis=-1, keepdims=True)
    i1 = jnp.min(jnp.where(lg == m1, lane_f, float(LANES)), axis=-1, keepdims=True)
    lg2 = jnp.where(lane_f == i1, -jnp.inf, lg)
    m2 = jnp.max(lg2, axis=-1, keepdims=True)
    i2 = jnp.min(jnp.where(lg2 == m2, lane_f, float(LANES)), axis=-1, keepdims=True)
    e = jnp.exp(m2 - m1)
    gate1 = 1.0 / (1.0 + e)
    gate2 = e / (1.0 + e)

    sel1 = lane_f == i1
    sel2 = lane_f == i2
    onehot = jnp.where(sel1, 1.0, jnp.where(sel2, 1.0, 0.0)).astype(BF16)
    tr = lax.broadcasted_iota(jnp.int32, (ts, ts), 0)
    tc = lax.broadcasted_iota(jnp.int32, (ts, ts), 1)
    lower = jnp.where(tc < tr, 1.0, 0.0).astype(BF16)
    before = jnp.dot(lower, onehot, preferred_element_type=F32) + carry[...]
    rank1 = jnp.sum(jnp.where(sel1, before, 0.0), axis=-1, keepdims=True)
    rank2 = jnp.sum(jnp.where(sel2, before, 0.0), axis=-1, keepdims=True)
    carry[...] = carry[...] + jnp.sum(onehot.astype(F32), axis=0, keepdims=True)
    cnt_ref[...] = jnp.broadcast_to(carry[...], cnt_ref.shape).astype(jnp.int32)

    r1 = rank1.astype(jnp.int32)
    r2 = rank2.astype(jnp.int32)
    mi_ref[...] = jnp.where(lane == 0, i1.astype(jnp.int32),
                            jnp.where(lane == 1, i2.astype(jnp.int32),
                                      jnp.where(lane == 2, r1, jnp.where(lane == 3, r2, 0))))
    mf_ref[...] = jnp.where(lane == 0, gate1, jnp.where(lane == 1, gate2, 0.0))


def _pool_router(h1, g1, pw, ps, g2, rw, bsz, seq):
    n = h1.shape[0]
    ns = seq // TS_POOL
    row = lambda b, s: (b * ns + s, 0)
    const = lambda b, s: (0, 0)
    return pl.pallas_call(
        _pool_router_kernel,
        grid=(bsz, ns),
        in_specs=[
            pl.BlockSpec((TS_POOL, D_MODEL), row),
            pl.BlockSpec((1, D_MODEL), const),
            pl.BlockSpec((len(POOL_WINDOWS), POOL_GROUP_DIM, POOL_GROUP_DIM),
                         lambda b, s: (0, 0, 0)),
            pl.BlockSpec((1, D_MODEL), const),
            pl.BlockSpec((1, D_MODEL), const),
            pl.BlockSpec((2, D_MODEL, LANES), lambda b, s: (0, 0, 0)),
        ],
        out_specs=[
            pl.BlockSpec((TS_POOL, D_MODEL), row),
            pl.BlockSpec((TS_POOL, LANES), row),
            pl.BlockSpec((TS_POOL, LANES), row),
            pl.BlockSpec((8, LANES), const),
        ],
        out_shape=[
            jax.ShapeDtypeStruct((n, D_MODEL), F32),
            jax.ShapeDtypeStruct((n, LANES), jnp.int32),
            jax.ShapeDtypeStruct((n, LANES), F32),
            jax.ShapeDtypeStruct((8, LANES), jnp.int32),
        ],
        scratch_shapes=[pltpu.VMEM((POOL_HALO + TS_POOL, D_MODEL), F32),
                        pltpu.VMEM((1, LANES), F32),
                        pltpu.VMEM((2, POOL_HALO + TS_POOL, POOL_GROUP_DIM), F32)],
        compiler_params=_params("arbitrary", "arbitrary"),
        name="pool_router",
    )(h1, g1, pw, ps, g2, rw)


def _dispatch_kernel(pos_ref, fill_ref, h_ref, xs_hbm, zbuf, sem, fsem):
    i = pl.program_id(0)
    td = TD_DISPATCH

    def issue(g, c):
        base = pl.multiple_of(g * ROW_UNROLL, ROW_UNROLL)
        for u in range(ROW_UNROLL):
            r = base + u
            src = h_ref.at[pl.ds(r, 1)]
            pltpu.make_async_copy(src, xs_hbm.at[pl.ds(pos_ref[0, 0, r], 1)], sem).start(
                priority=0)
            pltpu.make_async_copy(src, xs_hbm.at[pl.ds(pos_ref[0, 0, td + r], 1)], sem).start(
                priority=1)
        return c

    lax.fori_loop(0, td // ROW_UNROLL, issue, 0)

    @pl.when(i == pl.num_programs(0) - 1)
    def _():
        zbuf[...] = jnp.zeros_like(zbuf)
        zrow = zbuf.at[pl.ds(0, 1)]

        def row_copy(r):
            return pltpu.make_async_copy(zrow, xs_hbm.at[pl.ds(r, 1)], fsem)

        def tile_copy(t):
            start = pl.multiple_of(t * TM_EXP, TM_EXP)
            return pltpu.make_async_copy(zbuf, xs_hbm.at[pl.ds(start, TM_EXP)], fsem)

        def run(lo, hi, copy):
            def start(r, c):
                copy(r).start()
                return c

            def wait(r, c):
                copy(r).wait()
                return c

            lax.fori_loop(lo, hi, start, 0)
            lax.fori_loop(lo, hi, wait, 0)

        for e in range(N_EXPERTS):
            run(fill_ref[0, e], fill_ref[1, e], row_copy)
        run(fill_ref[2, 0], xs_hbm.shape[0] // TM_EXP, tile_copy)

    pltpu.make_async_copy(h_ref, xs_hbm.at[pl.ds(0, td)], sem).wait()
    pltpu.make_async_copy(h_ref, xs_hbm.at[pl.ds(0, td)], sem).wait()


def _dispatch(h2, pos3, fill, rows_padded):
    n = h2.shape[0]
    return pl.pallas_call(
        _dispatch_kernel,
        grid=(n // TD_DISPATCH,),
        in_specs=[
            pl.BlockSpec((1, 1, 2 * TD_DISPATCH), lambda i: (i, 0, 0), memory_space=pltpu.SMEM),
            pl.BlockSpec(memory_space=pltpu.SMEM),
            pl.BlockSpec((TD_DISPATCH, D_MODEL), lambda i: (i, 0)),
        ],
        out_specs=pl.BlockSpec(memory_space=pl.ANY),
        out_shape=jax.ShapeDtypeStruct((rows_padded, D_MODEL), F32),
        scratch_shapes=[pltpu.VMEM((TM_EXP, D_MODEL), F32), pltpu.SemaphoreType.DMA(()),
                        pltpu.SemaphoreType.DMA(())],
        compiler_params=_params("arbitrary"),
        name="dispatch",
    )(pos3, fill, h2)


def _experts_kernel(te_ref, tr_ref, na_ref, x_ref, g_ref, wg_ref, wu_ref, wd_ref, y_ref,
                    xn, acc):
    i = pl.program_id(0)
    j = pl.program_id(1)

    @pl.when(i < na_ref[0])
    def _():
        @pl.when(j == 0)
        def _():
            xn[...] = _rms(x_ref[...], g_ref[...], NORM_EPS).astype(BF16)
            acc[...] = jnp.zeros_like(acc)

        @pl.when(j < pl.num_programs(1) - 1)
        def _():
            acc[...] += _swiglu_chunk(xn[...], wg_ref, wu_ref, wd_ref)

        @pl.when(j == pl.num_programs(1) - 1)
        def _():
            y_ref[...] = acc[...] + _swiglu_chunk(xn[...], wg_ref, wu_ref, wd_ref)

    @pl.when((i >= na_ref[0]) & (j == 0))
    def _():
        y_ref[...] = jnp.zeros_like(y_ref)


def _experts(tile_expert, tile_row, n_active, xs, g, wg, wu, wd):
    rows = xs.shape[0]
    ntiles = rows // TM_EXP
    nf = EXPERT_DIM // TF_EXP

    def fcol(i, j, na):
        return jnp.where(i < na[0], j, nf - 1)

    grid_spec = pltpu.PrefetchScalarGridSpec(
        num_scalar_prefetch=3,
        grid=(ntiles, nf),
        in_specs=[
            pl.BlockSpec((TM_EXP, D_MODEL), lambda i, j, te, tr, na: (tr[i], 0)),
            pl.BlockSpec((1, D_MODEL), lambda i, j, te, tr, na: (0, 0)),
            pl.BlockSpec((None, D_MODEL, TF_EXP), lambda i, j, te, tr, na: (te[i], 0, fcol(i, j, na))),
            pl.BlockSpec((None, D_MODEL, TF_EXP), lambda i, j, te, tr, na: (te[i], 0, fcol(i, j, na))),
            pl.BlockSpec((None, TF_EXP, D_MODEL), lambda i, j, te, tr, na: (te[i], fcol(i, j, na), 0)),
        ],
        out_specs=pl.BlockSpec((TM_EXP, D_MODEL), lambda i, j, te, tr, na: (i, 0)),
        scratch_shapes=[pltpu.VMEM((TM_EXP, D_MODEL), BF16), pltpu.VMEM((TM_EXP, D_MODEL), F32)],
    )
    return pl.pallas_call(
        _experts_kernel,
        grid_spec=grid_spec,
        out_shape=jax.ShapeDtypeStruct((rows, D_MODEL), F32),
        compiler_params=_params("arbitrary", "arbitrary"),
        name="experts",
    )(tile_expert, tile_row, n_active, xs, g, wg, wu, wd)


def _combine_kernel(pos_ref, nxt_ref, h_ref, mf_ref, g_ref, y_hbm, o_ref, ybuf, sems):
    i = pl.program_id(0)
    tc = TC_COMBINE
    slot = i % 2

    def gather(p_ref, dst):
        def issue(g, c):
            base = pl.multiple_of(g * ROW_UNROLL, ROW_UNROLL)
            for u in range(ROW_UNROLL):
                r = base + u
                pltpu.make_async_copy(y_hbm.at[pl.ds(p_ref[0, 0, r], 1)],
                                      ybuf.at[dst, 0, pl.ds(r, 1)], sems.at[dst]).start(priority=0)
                pltpu.make_async_copy(y_hbm.at[pl.ds(p_ref[0, 0, tc + r], 1)],
                                      ybuf.at[dst, 1, pl.ds(r, 1)], sems.at[dst]).start(priority=1)
            return c

        lax.fori_loop(0, tc // ROW_UNROLL, issue, 0)

    @pl.when(i == 0)
    def _():
        gather(pos_ref, 0)

    @pl.when(i + 1 < pl.num_programs(0))
    def _():
        gather(nxt_ref, 1 - slot)

    blk = y_hbm.at[pl.ds(0, tc)]
    pltpu.make_async_copy(blk, ybuf.at[slot, 0], sems.at[slot]).wait()
    pltpu.make_async_copy(blk, ybuf.at[slot, 1], sems.at[slot]).wait()

    mf = mf_ref[...]
    out = h_ref[...] + mf[:, 0:1] * ybuf[slot, 0] + mf[:, 1:2] * ybuf[slot, 1]
    o_ref[...] = _rms(out, g_ref[...], NORM_EPS)


def _combine(pos3, h2, mf, g, y):
    n = h2.shape[0]
    nt = n // TC_COMBINE
    return pl.pallas_call(
        _combine_kernel,
        grid=(nt,),
        in_specs=[
            pl.BlockSpec((1, 1, 2 * TC_COMBINE), lambda i: (i, 0, 0), memory_space=pltpu.SMEM),
            pl.BlockSpec((1, 1, 2 * TC_COMBINE), lambda i: (jnp.minimum(i + 1, nt - 1), 0, 0),
                         memory_space=pltpu.SMEM),
            pl.BlockSpec((TC_COMBINE, D_MODEL), lambda i: (i, 0)),
            pl.BlockSpec((TC_COMBINE, LANES), lambda i: (i, 0)),
            pl.BlockSpec((1, D_MODEL), lambda i: (0, 0)),
            pl.BlockSpec(memory_space=pl.ANY),
        ],
        out_specs=pl.BlockSpec((TC_COMBINE, D_MODEL), lambda i: (i, 0)),
        out_shape=jax.ShapeDtypeStruct((n, D_MODEL), F32),
        scratch_shapes=[pltpu.VMEM((2, 2, TC_COMBINE, D_MODEL), F32),
                        pltpu.SemaphoreType.DMA((2,))],
        compiler_params=_params("arbitrary"),
        name="combine",
    )(pos3, pos3, h2, mf, g, y)


def _rope_tables(seq):
    half = ROPE_DIM // 2
    inv = ROPE_THETA ** (-jnp.arange(0, ROPE_DIM, 2, dtype=F32) / ROPE_DIM)
    ang = jnp.arange(seq, dtype=F32)[:, None] * inv[None, :]
    cos, sin = jnp.cos(ang), jnp.sin(ang)
    lane = jnp.arange(LANES) % A_QK_DIM
    idx = lane % half
    cs = jnp.where(lane < ROPE_DIM, cos[:, idx], 1.0)
    s1 = jnp.where(lane < half, -sin[:, idx], 0.0)
    s2 = jnp.where((lane >= half) & (lane < ROPE_DIM), sin[:, idx], 0.0)
    return cs, s1, s2


def _block_diag(w):
    heads, hd, _ = w.shape
    eye = jnp.eye(heads, dtype=w.dtype)
    return jnp.einsum("hij,hg->higj", w, eye).reshape(heads * hd, heads * hd)


def _routing_tables(meta_i, counts, n):
    tm = TM_EXP
    ntiles = (2 * n) // tm + N_EXPERTS
    cnt = counts[0, :N_EXPERTS]
    tiles_e = (cnt + tm - 1) // tm
    cum_tiles = jnp.cumsum(tiles_e)
    off = (cum_tiles - tiles_e) * tm
    pos1 = off[meta_i[:, 0]] + meta_i[:, 2]
    pos2 = off[meta_i[:, 1]] + meta_i[:, 3]
    n_active = cum_tiles[-1]
    t = jnp.arange(ntiles, dtype=jnp.int32)
    t_eff = jnp.minimum(t, n_active - 1)
    tile_expert = jnp.minimum(jnp.sum(t_eff[:, None] >= cum_tiles[None, :], axis=1),
                              N_EXPERTS - 1).astype(jnp.int32)
    fill = jnp.stack([off + cnt, off + tiles_e * tm,
                      jnp.broadcast_to(n_active, (N_EXPERTS,))]).astype(jnp.int32)
    return (pos1.astype(jnp.int32), pos2.astype(jnp.int32), tile_expert,
            t_eff.astype(jnp.int32), n_active.astype(jnp.int32).reshape(1), fill, ntiles * tm)


def _pos_blocks(pos1, pos2, blk):
    nb = pos1.shape[0] // blk
    return jnp.concatenate([pos1.reshape(nb, 1, blk), pos2.reshape(nb, 1, blk)], axis=-1)


def kernel(x, ln_mix_even, w_in_even, lam_q1, lam_k1, lam_q2, lam_k2, subln_g, conv_w, conv_b,
           rg_wa, rg_ba, rg_wx, rg_bx, rg_lam, w_out_even, ln_ffn_even, ffn_wg, ffn_wu, ffn_wd,
           ln_mix_odd, pool_w, pool_scale, ln_ffn_odd, router_w, moe_wg, moe_wu, moe_wd, final_g):
    bsz, seq, dm = x.shape
    n = bsz * seq
    x2 = x.reshape(n, dm)
    row = lambda v: v.reshape(1, -1)

    lam_init = 0.8 - 0.6 * math.exp(-0.3 * 0)
    cs, s1, s2 = _rope_tables(seq)
    proj, vt = _in_proj(x2, row(ln_mix_even[0]), w_in_even[0].astype(BF16), cs, s1, s2, seq)
    attn = _attention(proj, vt, row(lam_q1[0]), row(lam_k1[0]), row(lam_q2[0]), row(lam_k2[0]),
                      subln_g[0].reshape(-1, 1), bsz, seq, lam_init)
    w_gates = jnp.concatenate([_block_diag(rg_wa[0]), _block_diag(rg_wx[0])], axis=1).astype(BF16)
    b_gates = jnp.concatenate([rg_ba[0].reshape(1, -1), rg_bx[0].reshape(1, -1)], axis=1)
    rec = _rglru(proj, conv_w[0], row(conv_b[0]), w_gates, b_gates, row(rg_lam[0]), bsz, seq)
    h1 = _ffn(x2, attn, rec, w_out_even[0].astype(BF16), row(ln_ffn_even[0]),
              ffn_wg[0].astype(BF16), ffn_wu[0].astype(BF16), ffn_wd[0].astype(BF16))

    rw = jnp.pad(router_w[0], ((0, 0), (0, LANES - N_EXPERTS)))
    rw_hi = rw.astype(BF16)
    rw = jnp.stack([rw_hi, (rw - rw_hi.astype(F32)).astype(BF16)])
    h2, meta_i, meta_f, counts = _pool_router(
        h1, row(ln_mix_odd[0]), pool_w[0].astype(BF16), row(pool_scale[0]),
        row(ln_ffn_odd[0]), rw, bsz, seq)
    pos1, pos2, tile_expert, tile_row, n_active, fill, rows_padded = _routing_tables(
        meta_i, counts, n)
    xs = _dispatch(h2, _pos_blocks(pos1, pos2, TD_DISPATCH), fill, rows_padded)
    y = _experts(tile_expert, tile_row, n_active, xs, row(ln_ffn_odd[0]),
                 moe_wg[0], moe_wu[0], moe_wd[0])
    out = _combine(_pos_blocks(pos1, pos2, TC_COMBINE), h2, meta_f, row(final_g), y)
    return out.reshape(bsz, seq, dm)
```
